```python
import jax, jax.numpy as jnp
from jax import lax
import numpy as np

D_MODEL = 1024
BATCH = 2
SEQ = 16384
DEPTH = 1
DEC_BATCH = 128
DEC_SEQ = 8
PAST_LEN = 8192
PAGE_SIZE = 128

H_FOX = 8
DH_FOX = 64
D_FOX = H_FOX * DH_FOX
Q_BLOCK = 128
H_RET = 4
DK_RET = 64
DV_RET = 128
D_RET_QK = H_RET * DK_RET
D_RET_V = H_RET * DV_RET
RET_CHUNK = 128
ROPE_BASE = 10000.0
N_GROUPS = 4
EXPERTS_PER_GROUP = 8
D_EXPERT = 256
TOP_K = 2
EPS = 1e-6
D_IN_PROJ = 3 * D_FOX + H_FOX + 2 * D_RET_QK + 2 * D_RET_V + 2 * D_MODEL

kernel_name = 'fox_retention_hmoe_adaln_step'


def _rmsnorm(x, g):
    xf = x.astype(jnp.float32)
    y = xf * lax.rsqrt(jnp.mean(xf * xf, axis=-1, keepdims=True) + EPS)
    return (y * g.astype(jnp.float32)).astype(x.dtype)


def _head_norm(y):
    yf = y.astype(jnp.float32)
    mu = jnp.mean(yf, axis=-1, keepdims=True)
    var = jnp.mean(jnp.square(yf - mu), axis=-1, keepdims=True)
    return (yf - mu) * lax.rsqrt(var + EPS)


def _ada_mod(c, w_ada, b_ada):
    m = jax.nn.silu(c) @ w_ada + b_ada
    return jnp.split(m[:, None, :], 6, axis=-1)


def _rope(x, pos):
    half = x.shape[-1] // 2
    inv = ROPE_BASE ** (-jnp.arange(half, dtype=jnp.float32) / half)
    ang = pos.astype(jnp.float32)[:, None] * inv[None, :]
    cos = jnp.cos(ang)[None, :, None, :].astype(x.dtype)
    sin = jnp.sin(ang)[None, :, None, :].astype(x.dtype)
    x1, x2 = x[..., :half], x[..., half:]
    return jnp.concatenate([x1 * cos - x2 * sin, x1 * sin + x2 * cos], axis=-1)


def _ret_log_gamma():
    return jnp.log1p(-jnp.exp2(-5.0 - jnp.arange(H_RET, dtype=jnp.float32)))


def _in_proj(h, w_in, b_fgt, pos):
    B, L = h.shape[:2]
    p = h @ w_in
    sizes = [D_FOX, D_FOX, D_FOX, H_FOX, D_RET_QK, D_RET_QK, D_RET_V, D_RET_V, D_MODEL, D_MODEL]
    idx = [int(i) for i in np.cumsum(sizes[:-1])]
    qa, ka, va, fa, qb, kb, vb, rg, ga, gb = jnp.split(p, idx, axis=-1)
    qa = qa.reshape(B, L, H_FOX, DH_FOX)
    ka = ka.reshape(B, L, H_FOX, DH_FOX)
    va = va.reshape(B, L, H_FOX, DH_FOX)
    logf = jax.nn.log_sigmoid(fa.astype(jnp.float32) + b_fgt.astype(jnp.float32))
    qb = _rope(qb.reshape(B, L, H_RET, DK_RET), pos)
    kb = _rope(kb.reshape(B, L, H_RET, DK_RET), pos) * (DK_RET ** -0.5)
    vb = vb.reshape(B, L, H_RET, DV_RET)
    return qa, ka, va, logf, qb, kb, vb, rg, ga, gb


def _fox_prompt(q, k, v, logf):
    B, S = q.shape[:2]
    scale = DH_FOX ** -0.5
    c_t = lax.cumsum(logf, axis=1).transpose(0, 2, 1)
    key_pos = jnp.arange(S)

    def block(i):
        start = i * Q_BLOCK
        qb = lax.dynamic_slice_in_dim(q, start, Q_BLOCK, axis=1)
        cq = lax.dynamic_slice_in_dim(c_t, start, Q_BLOCK, axis=2)
        s = jnp.einsum('bqhd,bkhd->bhqk', qb, k, preferred_element_type=jnp.float32) * scale
        s = s + (cq[..., :, None] - c_t[:, :, None, :])
        q_pos = start + jnp.arange(Q_BLOCK)
        s = jnp.where(key_pos[None, :] <= q_pos[:, None], s, -jnp.inf)
        p = jax.nn.softmax(s, axis=-1).astype(v.dtype)
        return jnp.einsum('bhqk,bkhd->bqhd', p, v)

    out = lax.map(block, jnp.arange(S // Q_BLOCK))
    return out.transpose(1, 0, 2, 3, 4).reshape(B, S, D_FOX)


def _online_softmax_step(carry, s, v):
    m, l, acc = carry
    m_new = jnp.maximum(m, jnp.max(s, axis=-1))
    corr = jnp.exp(m - m_new)
    p = jnp.exp(s - m_new[..., None])
    l = l * corr + jnp.sum(p, axis=-1)
    acc = acc * corr[..., None] + jnp.einsum('bhqk,bkhd->bhqd', p, v.astype(jnp.float32))
    return (m_new, l, acc)


def _fox_sample(q, k_new, v_new, logf_new, cache_k, cache_v, cache_logf, page_table, layer):
    Bd, L = q.shape[:2]
    n_pages = page_table.shape[1]
    scale = DH_FOX ** -0.5
    lf = cache_logf[layer, page_table].astype(jnp.float32).reshape(Bd, n_pages * PAGE_SIZE, H_FOX)
    suffix = lax.cumsum(lf, axis=1, reverse=True) - lf
    suffix = suffix.reshape(Bd, n_pages, PAGE_SIZE, H_FOX).transpose(1, 0, 3, 2)
    c_new = lax.cumsum(logf_new, axis=1).transpose(0, 2, 1)
    carry0 = (jnp.full((Bd, H_FOX, L), -jnp.inf, jnp.float32),
              jnp.zeros((Bd, H_FOX, L), jnp.float32),
              jnp.zeros((Bd, H_FOX, L, DH_FOX), jnp.float32))

    def page_step(carry, xs):
        pt, suf = xs
        kp = cache_k[layer, pt]
        vp = cache_v[layer, pt]
        s = jnp.einsum('bqhd,bkhd->bhqk', q, kp, preferred_element_type=jnp.float32) * scale
        s = s + c_new[..., :, None] + suf[:, :, None, :]
        return _online_softmax_step(carry, s, vp), None

    carry, _ = lax.scan(page_step, carry0, (page_table.T, suffix))
    s = jnp.einsum('bqhd,bkhd->bhqk', q, k_new, preferred_element_type=jnp.float32) * scale
    s = s + (c_new[..., :, None] - c_new[..., None, :])
    causal = jnp.tril(jnp.ones((L, L), dtype=bool))
    s = jnp.where(causal, s, -jnp.inf)
    m, l, acc = _online_softmax_step(carry, s, v_new)
    out = (acc / l[..., None]).transpose(0, 2, 1, 3).reshape(Bd, L, D_FOX)
    return out.astype(q.dtype)


def _retention(q, k, v, r0, chunk):
    B, L = q.shape[:2]
    nc = L // chunk
    lg = _ret_log_gamma()

    def to_chunks(t):
        d = t.shape[-1]
        return t.astype(jnp.float32).transpose(0, 2, 1, 3).reshape(B, H_RET, nc, chunk, d).transpose(2, 0, 1, 3, 4)

    pos = jnp.arange(chunk, dtype=jnp.float32)
    diff = pos[:, None] - pos[None, :]
    decay = jnp.where(diff >= 0, jnp.exp(lg[:, None, None] * jnp.maximum(diff, 0.0)), 0.0)
    q_dec = jnp.exp(lg[:, None] * (pos + 1.0))
    k_dec = jnp.exp(lg[:, None] * (chunk - 1.0 - pos))
    chunk_dec = jnp.exp(lg * chunk)

    def step(r, qkv):
        qc, kc, vc = qkv
        inner = jnp.einsum('bhqd,bhkd->bhqk', qc, kc) * decay
        out = jnp.einsum('bhqk,bhkv->bhqv', inner, vc) + q_dec[:, :, None] * jnp.einsum('bhqd,bhdv->bhqv', qc, r)
        r = chunk_dec[:, None, None] * r + jnp.einsum('bhkd,bhkv->bhdv', kc * k_dec[:, :, None], vc)
        return r, out

    r, outs = lax.scan(step, r0.astype(jnp.float32), (to_chunks(q), to_chunks(k), to_chunks(v)))
    out = outs.transpose(1, 0, 3, 2, 4).reshape(B, L, H_RET, DV_RET)
    return out, r


def _hmoe(h, w_rg, b_rg, w_re, b_re, w_eg, w_eu, w_ed):
    B, L, D = h.shape
    t = h.reshape(B * L, D)
    lg_g = (t @ w_rg + b_rg).astype(jnp.float32)
    p_g = jax.nn.softmax(lg_g, axis=-1)
    g_idx = jnp.argmax(lg_g, axis=-1)
    g_w = jnp.max(p_g, axis=-1, keepdims=True)
    onehot_g = jax.nn.one_hot(g_idx, N_GROUPS, dtype=jnp.float32)
    lg_e = (t @ w_re + b_re).astype(jnp.float32).reshape(-1, N_GROUPS, EXPERTS_PER_GROUP)
    lg_sel = jnp.einsum('tg,tge->te', onehot_g, lg_e)
    top_v, top_i = lax.top_k(lg_sel, TOP_K)
    top_w = jax.nn.softmax(top_v, axis=-1) * g_w
    fine = jnp.sum(jax.nn.one_hot(top_i, EXPERTS_PER_GROUP, dtype=jnp.float32) * top_w[..., None], axis=1)
    combine = (onehot_g[:, :, None] * fine[:, None, :]).astype(t.dtype)
    out = jnp.zeros_like(t)
    for g in range(N_GROUPS):
        a = jnp.einsum('td,edf->tef', t, w_eg[g])
        u = jnp.einsum('td,edf->tef', t, w_eu[g])
        hid = jax.nn.silu(a) * u * combine[:, g, :, None]
        out = out + jnp.einsum('tef,efd->td', hid, w_ed[g])
    return out.reshape(B, L, D)


def _trunk_layer(x, c, pos, attend, r0, ret_chunk, w_ada, b_ada, norm_mix, norm_ffn, w_in, b_fgt,
                 w_br_fox, w_br_ret, w_out, w_rg, b_rg, w_re, b_re, w_eg, w_eu, w_ed):
    B, L = x.shape[:2]
    sh1, sc1, g1, sh2, sc2, g2 = _ada_mod(c, w_ada, b_ada)
    h = _rmsnorm(x, norm_mix) * (1.0 + sc1) + sh1
    qa, ka, va, logf, qb, kb, vb, rg, ga, gb = _in_proj(h, w_in, b_fgt, pos)
    y_fox = attend(qa, ka, va, logf)
    y_ret, r_new = _retention(qb, kb, vb, r0, ret_chunk)
    y_ret = _head_norm(y_ret).reshape(B, L, D_RET_V).astype(x.dtype) * jax.nn.silu(rg)
    mixed = jax.nn.sigmoid(ga) * (y_fox @ w_br_fox) + jax.nn.sigmoid(gb) * (y_ret @ w_br_ret)
    x = x + g1 * (mixed @ w_out)
    h2 = _rmsnorm(x, norm_ffn) * (1.0 + sc2) + sh2
    x = x + g2 * _hmoe(h2, w_rg, b_rg, w_re, b_re, w_eg, w_eu, w_ed)
    return x, ka, va, logf, r_new


def setup_inputs(seed: int = 0) -> dict:
    key = jax.random.key(seed)
    ks = jax.random.split(key, 28)
    f32 = jnp.float32

    def nrm(k, shape, scale=1.0):
        return jax.random.normal(k, shape, f32) * scale

    n_pages = PAST_LEN // PAGE_SIZE
    n_used = DEC_BATCH * n_pages
    n_pool = n_used + (n_used + 3) // 4
    page_table = jax.random.permutation(ks[0], n_pool)[:n_used].reshape(DEC_BATCH, n_pages).astype(jnp.int32)
    D = D_MODEL
    return {
        'x_prompt': nrm(ks[1], (BATCH, SEQ, D)),
        'x_sample': nrm(ks[2], (DEC_BATCH, DEC_SEQ, D)),
        'cache_k': nrm(ks[3], (DEPTH, n_pool, PAGE_SIZE, H_FOX, DH_FOX)),
        'cache_v': nrm(ks[4], (DEPTH, n_pool, PAGE_SIZE, H_FOX, DH_FOX)),
        'cache_logf': jax.nn.log_sigmoid(3.0 + nrm(ks[5], (DEPTH, n_pool, PAGE_SIZE, H_FOX), 0.5)),
        'state_ret': nrm(ks[6], (DEPTH, DEC_BATCH, H_RET, DK_RET, DV_RET)),
        'page_table': page_table,
        'c_prompt': nrm(ks[7], (BATCH, D)),
        'c_sample': nrm(ks[8], (DEC_BATCH, D)),
        'w_ada': nrm(ks[9], (DEPTH, D, 6 * D), 0.5 * D ** -0.5),
        'b_ada': nrm(ks[10], (DEPTH, 6 * D), 0.01),
        'norm_mix': 1.0 + nrm(ks[11], (DEPTH, D), 0.1),
        'norm_ffn': 1.0 + nrm(ks[12], (DEPTH, D), 0.1),
        'w_in': nrm(ks[13], (DEPTH, D, D_IN_PROJ), D ** -0.5),
        'b_fgt': 3.0 + nrm(ks[14], (DEPTH, H_FOX), 0.5),
        'w_br_fox': nrm(ks[15], (DEPTH, D_FOX, D), D_FOX ** -0.5),
        'w_br_ret': nrm(ks[16], (DEPTH, D_RET_V, D), D_RET_V ** -0.5),
        'w_out': nrm(ks[17], (DEPTH, D, D), D ** -0.5),
        'w_route_group': nrm(ks[18], (DEPTH, D, N_GROUPS), D ** -0.5),
        'b_route_group': nrm(ks[19], (DEPTH, N_GROUPS), 0.01),
        'w_route_expert': nrm(ks[20], (DEPTH, D, N_GROUPS * EXPERTS_PER_GROUP), D ** -0.5),
        'b_route_expert': nrm(ks[21], (DEPTH, N_GROUPS * EXPERTS_PER_GROUP), 0.01),
        'w_exp_gate': nrm(ks[22], (DEPTH, N_GROUPS, EXPERTS_PER_GROUP, D, D_EXPERT), D ** -0.5),
        'w_exp_up': nrm(ks[23], (DEPTH, N_GROUPS, EXPERTS_PER_GROUP, D, D_EXPERT), D ** -0.5),
        'w_exp_down': nrm(ks[24], (DEPTH, N_GROUPS, EXPERTS_PER_GROUP, D_EXPERT, D), D_EXPERT ** -0.5),
        'norm_final': 1.0 + nrm(ks[25], (D,), 0.1),
    }


def reference(x_prompt, x_sample, cache_k, cache_v, cache_logf, state_ret, page_table, c_prompt, c_sample,
              w_ada, b_ada, norm_mix, norm_ffn, w_in, b_fgt, w_br_fox, w_br_ret, w_out,
              w_route_group, b_route_group, w_route_expert, b_route_expert,
              w_exp_gate, w_exp_up, w_exp_down, norm_final):
    pos_p = jnp.arange(SEQ)
    pos_s = PAST_LEN + jnp.arange(DEC_SEQ)
    xp, xs = x_prompt, x_sample
    kp_l, vp_l, fp_l, rp_l = [], [], [], []
    ks_l, vs_l, fs_l, rs_l = [], [], [], []
    for l in range(DEPTH):
        w_layer = (w_ada[l], b_ada[l], norm_mix[l], norm_ffn[l], w_in[l], b_fgt[l], w_br_fox[l], w_br_ret[l],
                   w_out[l], w_route_group[l], b_route_group[l], w_route_expert[l], b_route_expert[l],
                   w_exp_gate[l], w_exp_up[l], w_exp_down[l])
        r0 = jnp.zeros((xp.shape[0], H_RET, DK_RET, DV_RET), jnp.float32)
        xp, kp, vp, fp, rp = _trunk_layer(xp, c_prompt, pos_p, _fox_prompt, r0, RET_CHUNK, *w_layer)
        attend_s = lambda q, k, v, f, layer=l: _fox_sample(q, k, v, f, cache_k, cache_v, cache_logf, page_table, layer)
        xs, ksm, vsm, fsm, rsm = _trunk_layer(xs, c_sample, pos_s, attend_s, state_ret[l], DEC_SEQ, *w_layer)
        kp_l.append(kp); vp_l.append(vp); fp_l.append(fp); rp_l.append(rp)
        ks_l.append(ksm); vs_l.append(vsm); fs_l.append(fsm); rs_l.append(rsm)
    y_prompt = _rmsnorm(xp, norm_final)
    y_sample = _rmsnorm(xs, norm_final)
    k_prompt = jnp.stack(kp_l, axis=0)
    v_prompt = jnp.stack(vp_l, axis=0)
    logf_prompt = jnp.stack(fp_l, axis=0)
    ret_state_prompt = jnp.stack(rp_l, axis=0)
    k_sample = jnp.stack(ks_l, axis=0)
    v_sample = jnp.stack(vs_l, axis=0)
    logf_sample = jnp.stack(fs_l, axis=0)
    ret_state_sample = jnp.stack(rs_l, axis=0)
    return (y_prompt, y_sample, k_prompt, v_prompt, logf_prompt, ret_state_prompt,
            k_sample, v_sample, logf_sample, ret_state_sample)
```

```python
import functools

import numpy as np
import jax
import jax.numpy as jnp
from jax import lax
from jax.experimental import pallas as pl
from jax.experimental.pallas import tpu as pltpu

F32 = jnp.float32
BF16 = jnp.bfloat16

D_MODEL = 1024
H_FOX, DH_FOX = 8, 64
D_FOX = H_FOX * DH_FOX
H_RET, DK_RET, DV_RET = 4, 64, 128
D_RET_QK = H_RET * DK_RET
D_RET_V = H_RET * DV_RET
RET_CHUNK = 128
ROPE_BASE = 10000.0
N_GROUPS, EXPERTS_PER_GROUP, D_EXPERT = 4, 8, 256
N_EXPERTS = N_GROUPS * EXPERTS_PER_GROUP
EPS = 1e-6
LOG2E = 1.4426950408889634
NEG_BIG = -1e30
LANES = 128

_C_QA, _C_KA, _C_VA = 0, 512, 1024
_C_QB, _C_KB, _C_VB, _C_RG, _C_GA, _C_GB, _C_FA, _C_END = 1536, 1792, 2048, 2560, 3072, 4096, 5120, 5248
_R_EXP0 = N_GROUPS

VMEM_LIMIT = 56 * 1024 * 1024


def _cparams(sem, vmem=VMEM_LIMIT):
    return pltpu.CompilerParams(dimension_semantics=sem, vmem_limit_bytes=vmem)


def _split3(x):
    hi = x.astype(BF16)
    r1 = x - hi.astype(F32)
    mid = r1.astype(BF16)
    lo = (r1 - mid.astype(F32)).astype(BF16)
    return hi, mid, lo


def _dot(a, b):
    return jnp.dot(a, b, preferred_element_type=F32)


def _dot_nt(a, b):
    return lax.dot_general(a, b, (((1,), (1,)), ((), ())), preferred_element_type=F32)


def _dot_tn(a, b):
    return lax.dot_general(a, b, (((0,), (0,)), ((), ())), preferred_element_type=F32)


def _dot3_rhs_exact(x, m):
    return sum(_dot(p, m) for p in _split3(x))


def _dot3_lhs_exact(m, x):
    return sum(_dot(m, p) for p in _split3(x))


def _ada_kernel(c_ref, w_ref, b_ref, o_ref):
    a = jax.nn.silu(c_ref[...]).astype(BF16)
    o_ref[...] = _dot(a, w_ref[...].astype(BF16)) + b_ref[...]


def _ada_mod(c_all, w_ada, b_ada):
    rows = c_all.shape[0]
    n = w_ada.shape[1]
    tn = 1536
    return pl.pallas_call(
        _ada_kernel,
        grid=(n // tn,),
        in_specs=[pl.BlockSpec((rows, D_MODEL), lambda j: (0, 0)),
                  pl.BlockSpec((D_MODEL, tn), lambda j: (0, j)),
                  pl.BlockSpec((1, tn), lambda j: (0, j))],
        out_specs=pl.BlockSpec((rows, tn), lambda j: (0, j)),
        out_shape=jax.ShapeDtypeStruct((rows, n), F32),
        name="ada_mod",
        compiler_params=_cparams(("parallel",)),
    )(c_all, w_ada, b_ada.reshape(1, n))


def _inproj_kernel(x_ref, sc_ref, sh_ref, g_ref, w_ref, bf_ref, cos_ref, sa_ref, sb_ref,
                   q_ref, k_ref, v_ref, lf_ref, qb_ref, kb_ref, vb_ref, rg_ref, ga_ref, gb_ref):
    x = x_ref[...]
    y = x * lax.rsqrt(jnp.mean(x * x, axis=-1, keepdims=True) + EPS) * g_ref[...]
    h = (y * (1.0 + sc_ref[0]) + sh_ref[0]).astype(BF16)

    def proj(lo, hi):
        return _dot(h, w_ref[:, lo:hi])

    cos, sa, sb = cos_ref[...], sa_ref[...], sb_ref[...]

    def rope(t):
        return t * cos + pltpu.roll(t, D_RET_QK - DK_RET // 2, 1) * sa + pltpu.roll(t, DK_RET // 2, 1) * sb

    q_ref[...] = (proj(_C_QA, _C_KA) * (DH_FOX ** -0.5 * LOG2E)).astype(BF16)
    k_ref[...] = proj(_C_KA, _C_VA)
    v_ref[...] = proj(_C_VA, _C_QB)
    qb_ref[...] = rope(proj(_C_QB, _C_KB)).astype(BF16)
    kb_ref[...] = (rope(proj(_C_KB, _C_VB)) * DK_RET ** -0.5).astype(BF16)
    vb_ref[...] = proj(_C_VB, _C_RG).astype(BF16)
    rg_ref[...] = proj(_C_RG, _C_GA)
    ga_ref[...] = proj(_C_GA, _C_GB)
    gb_ref[...] = proj(_C_GB, _C_FA)
    lf_ref[...] = jax.nn.log_sigmoid(proj(_C_FA, _C_END) + bf_ref[...])


def _mod_spec(mod, tiles_per_batch):
    r = mod.shape[1]
    if tiles_per_batch is None:
        return pl.BlockSpec((1, r, D_MODEL), lambda i: (i, 0, 0))
    return pl.BlockSpec((1, r, D_MODEL), lambda i: (i // tiles_per_batch, 0, 0))


def _in_proj(x, sc, sh, gain, w_perm, b_fgt_pad, cos, sa, sb, tm, tiles_per_batch, rope_tiles):
    t = x.shape[0]
    row = lambda n: pl.BlockSpec((tm, n), lambda i: (i, 0))
    const = lambda shp: pl.BlockSpec(shp, lambda i: (0,) * len(shp))
    rope_spec = pl.BlockSpec((tm, D_RET_QK), lambda i: (i % rope_tiles, 0))
    outs = [(D_FOX, BF16), (D_FOX, F32), (D_FOX, F32), (LANES, F32), (D_RET_QK, BF16), (D_RET_QK, BF16),
            (D_RET_V, BF16), (D_RET_V, F32), (D_MODEL, F32), (D_MODEL, F32)]
    return pl.pallas_call(
        _inproj_kernel,
        grid=(t // tm,),
        in_specs=[row(D_MODEL), _mod_spec(sc, tiles_per_batch), _mod_spec(sh, tiles_per_batch),
                  const((1, D_MODEL)), const((D_MODEL, _C_END)), const((1, LANES)),
                  rope_spec, rope_spec, rope_spec],
        out_specs=[row(n) for n, _ in outs],
        out_shape=[jax.ShapeDtypeStruct((t, n), dt) for n, dt in outs],
        name="in_proj",
        compiler_params=_cparams(("parallel",)),
    )(x, sc, sh, gain, w_perm, b_fgt_pad, cos, sa, sb)


def _cumsum_kernel(x_ref, hi_ref, mid_ref, lo_ref):
    x = x_ref[0]
    r = x.shape[0]
    a = lax.broadcasted_iota(jnp.int32, (LANES, LANES), 0)
    b = lax.broadcasted_iota(jnp.int32, (LANES, LANES), 1)
    incl = (a <= b).astype(BF16)
    y = _dot3_rhs_exact(x, incl)
    tot = jnp.broadcast_to(y[:, LANES - 1:LANES], (r, LANES))
    ra = lax.broadcasted_iota(jnp.int32, (r, r), 0)
    rb = lax.broadcasted_iota(jnp.int32, (r, r), 1)
    strict = (rb < ra).astype(BF16)
    c = (y + _dot3_lhs_exact(strict, tot)) * LOG2E
    hi, mid, lo = _split3(c)
    hi_ref[0] = hi
    mid_ref[0] = mid
    lo_ref[0] = lo


def _cumsum_split(lf_rows):
    n, r, _ = lf_rows.shape
    spec = pl.BlockSpec((1, r, LANES), lambda i: (i, 0, 0))
    return pl.pallas_call(
        _cumsum_kernel,
        grid=(n,),
        in_specs=[spec],
        out_specs=[spec] * 3,
        out_shape=[jax.ShapeDtypeStruct((n, r, LANES), BF16)] * 3,
        name="logf_cumsum",
        compiler_params=_cparams(("parallel",)),
    )(lf_rows)


def _fox_prompt_kernel(q_ref, k_ref, v_ref, o_ref, *, tq, tk):
    i = pl.program_id(2)
    qt = q_ref[0, 0]
    per = tq // tk

    def step(j, carry, masked):
        m, l, acc = carry
        s = _dot(k_ref[0, 0, j], qt)
        if masked:
            kpos = j * tk + lax.broadcasted_iota(jnp.int32, (tk, tq), 0)
            qpos = i * tq + lax.broadcasted_iota(jnp.int32, (tk, tq), 1)
            s = jnp.where(kpos <= qpos, s, NEG_BIG)
        m_new = jnp.maximum(m, jnp.max(s, axis=0, keepdims=True))
        alpha = jnp.exp2(m - m_new)
        p = jnp.exp2(s - m_new)
        l = alpha * l + jnp.sum(p, axis=0, keepdims=True)
        acc = alpha * acc + _dot(v_ref[0, 0, j], p.astype(BF16))
        return m_new, l, acc

    init = (jnp.full((1, tq), NEG_BIG, F32), jnp.zeros((1, tq), F32), jnp.zeros((DH_FOX, tq), F32))
    carry = lax.fori_loop(0, i * per, lambda j, c: step(j, c, False), init)
    for jj in range(per):
        carry = step(i * per + jj, carry, True)
    _, l, acc = carry
    o_ref[0, 0] = (acc / l).astype(BF16)


def _fox_prompt(qt_aug, k_aug, vt, tq, tk):
    b, h, _, s = qt_aug.shape
    nk = s // tk
    return pl.pallas_call(
        functools.partial(_fox_prompt_kernel, tq=tq, tk=tk),
        grid=(b, h, s // tq),
        in_specs=[pl.BlockSpec((1, 1, LANES, tq), lambda bi, hi, i: (bi, hi, 0, i)),
                  pl.BlockSpec((1, 1, nk, tk, LANES), lambda bi, hi, i: (bi, hi, 0, 0, 0)),
                  pl.BlockSpec((1, 1, nk, DH_FOX, tk), lambda bi, hi, i: (bi, hi, 0, 0, 0))],
        out_specs=pl.BlockSpec((1, 1, DH_FOX, tq), lambda bi, hi, i: (bi, hi, 0, i)),
        out_shape=jax.ShapeDtypeStruct((b, h, DH_FOX, s), BF16),
        name="fox_prompt",
        compiler_params=_cparams(("parallel", "parallel", "arbitrary")),
    )(qt_aug, k_aug, vt)


def _fox_sample_kernel(pt_ref, *refs, pps, n_steps, lq):
    k_refs = refs[:pps]
    v_refs = refs[pps:2 * pps]
    f_refs = refs[2 * pps:3 * pps]
    qt_ref, kn_ref, vn_ref, fn_ref, o_ref, m_scr, l_scr, acc_scr, car_scr = refs[3 * pps:]
    del pt_ref
    hq = H_FOX * lq
    step = pl.program_id(1)

    row_h = lax.broadcasted_iota(jnp.int32, (hq, H_FOX), 0) // lq
    col_h = lax.broadcasted_iota(jnp.int32, (hq, H_FOX), 1)
    rep = (row_h == col_h).astype(BF16)
    ja = lax.broadcasted_iota(jnp.int32, (LANES, LANES), 0)
    jb = lax.broadcasted_iota(jnp.int32, (LANES, LANES), 1)
    later = (ja > jb).astype(BF16)

    xn = sum(_dot_nt(rep, p) for p in _split3(fn_ref[0]))
    na = lax.broadcasted_iota(jnp.int32, (lq, lq), 0)
    nb = lax.broadcasted_iota(jnp.int32, (lq, lq), 1)
    cj = _dot3_rhs_exact(xn, (na <= nb).astype(BF16)) * LOG2E
    qi = lax.broadcasted_iota(jnp.int32, (hq, lq), 0) % lq
    kj = lax.broadcasted_iota(jnp.int32, (hq, lq), 1)
    c_q = jnp.sum(jnp.where(qi == kj, cj, 0.0), axis=-1, keepdims=True)

    @pl.when(step == 0)
    def _():
        m_scr[...] = jnp.full(m_scr.shape, NEG_BIG, F32)
        l_scr[...] = jnp.zeros(l_scr.shape, F32)
        acc_scr[...] = jnp.zeros(acc_scr.shape, F32)
        car_scr[...] = jnp.zeros(car_scr.shape, F32)

    qt = qt_ref[0]

    def update(s, v):
        m = m_scr[...]
        m_new = jnp.maximum(m, jnp.max(s, axis=-1, keepdims=True))
        alpha = jnp.exp2(m - m_new)
        p = jnp.exp2(s - m_new)
        l_scr[...] = alpha * l_scr[...] + jnp.sum(p, axis=-1, keepdims=True)
        acc_scr[...] = alpha * acc_scr[...] + _dot(p.astype(BF16), v)
        m_scr[...] = m_new

    for j in range(pps):
        x = sum(_dot_nt(rep, p) for p in _split3(f_refs[j][0]))
        suffix = _dot3_rhs_exact(x, later) + car_scr[...]
        s = _dot_nt(qt, k_refs[j][0].astype(BF16)) + (suffix * LOG2E + c_q)
        update(s, v_refs[j][0].astype(BF16))
        car_scr[...] = car_scr[...] + jnp.sum(x, axis=-1, keepdims=True)

    @pl.when(step == n_steps - 1)
    def _():
        s = _dot_nt(qt, kn_ref[0]) + (c_q - cj)
        s = jnp.where(kj <= qi, s, NEG_BIG)
        update(s, vn_ref[0])
        acc = acc_scr[...] / l_scr[...]
        rh = lax.broadcasted_iota(jnp.int32, (hq, D_FOX), 0) // lq
        ch = lax.broadcasted_iota(jnp.int32, (hq, D_FOX), 1) // DH_FOX
        acc = jnp.where(rh == ch, acc, 0.0)
        out = acc[0:lq]
        for h in range(1, H_FOX):
            out = out + acc[h * lq:(h + 1) * lq]
        o_ref[0] = out.astype(BF16)


def _fox_sample(page_table, cache_k, cache_v, cache_f, qt_bd, k_new, v_new, f_new, pps):
    bd, n_pages = page_table.shape
    lq = k_new.shape[1]
    hq = H_FOX * lq
    n_steps = n_pages // pps
    ps = cache_k.shape[1]

    def page_map(j):
        return lambda b, p, pt: (pt[b, n_pages - 1 - (p * pps + j)], 0, 0)

    per_b = lambda shp: pl.BlockSpec((1,) + shp, lambda b, p, pt: (b, 0, 0))
    in_specs = ([pl.BlockSpec((1, ps, D_FOX), page_map(j)) for j in range(pps)]
                + [pl.BlockSpec((1, ps, D_FOX), page_map(j)) for j in range(pps)]
                + [pl.BlockSpec((1, ps, H_FOX), page_map(j)) for j in range(pps)]
                + [per_b((hq, D_FOX)), per_b((lq, D_FOX)), per_b((lq, D_FOX)), per_b((lq, H_FOX))])
    grid_spec = pltpu.PrefetchScalarGridSpec(
        num_scalar_prefetch=1,
        grid=(bd, n_steps),
        in_specs=in_specs,
        out_specs=per_b((lq, D_FOX)),
        scratch_shapes=[pltpu.VMEM((hq, 1), F32), pltpu.VMEM((hq, 1), F32),
                        pltpu.VMEM((hq, D_FOX), F32), pltpu.VMEM((hq, 1), F32)],
    )
    return pl.pallas_call(
        functools.partial(_fox_sample_kernel, pps=pps, n_steps=n_steps, lq=lq),
        grid_spec=grid_spec,
        out_shape=jax.ShapeDtypeStruct((bd, lq, D_FOX), BF16),
        name="fox_sample",
        compiler_params=_cparams(("parallel", "arbitrary")),
    )(page_table, *([cache_k] * pps), *([cache_v] * pps), *([cache_f] * pps), qt_bd, k_new, v_new, f_new)


def _ret_head(q, k, v, rg, r, dec, qd, kd, cd):
    mm = q.dtype
    inner = _dot_nt(q, k) * dec
    o = _dot(inner.astype(mm), v) + qd * _dot(q, r.astype(mm))
    r_new = cd * r + _dot_tn((k.astype(F32) * kd).astype(mm), v)
    mu = jnp.mean(o, axis=-1, keepdims=True)
    var = jnp.mean(jnp.square(o - mu), axis=-1, keepdims=True)
    y = (o - mu) * lax.rsqrt(var + EPS)
    return (y * jax.nn.silu(rg)).astype(BF16), r_new


def _ret_prompt_kernel(q_ref, k_ref, v_ref, rg_ref, dec_ref, qd_ref, kd_ref, cd_ref, y_ref, rout_ref, r_scr,
                       *, chunk, n_chunks):
    step = pl.program_id(1)

    @pl.when(step == 0)
    def _():
        r_scr[...] = jnp.zeros(r_scr.shape, F32)

    for c in range(n_chunks):
        rows = slice(c * chunk, (c + 1) * chunk)
        for h in range(H_RET):
            qk = slice(h * DK_RET, (h + 1) * DK_RET)
            vv = slice(h * DV_RET, (h + 1) * DV_RET)
            y, r_new = _ret_head(q_ref[rows, qk], k_ref[rows, qk], v_ref[rows, vv], rg_ref[rows, vv], r_scr[h],
                                 dec_ref[h], qd_ref[h], kd_ref[h], cd_ref[h])
            y_ref[rows, vv] = y
            r_scr[h] = r_new

    @pl.when(step == pl.num_programs(1) - 1)
    def _():
        rout_ref[0] = r_scr[...]


def _ret_tables(chunk):
    lg = jnp.log1p(-jnp.exp2(-5.0 - jnp.arange(H_RET, dtype=F32)))
    pos = jnp.arange(chunk, dtype=F32)
    diff = pos[:, None] - pos[None, :]
    dec = jnp.where(diff >= 0, jnp.exp(lg[:, None, None] * jnp.maximum(diff, 0.0)), 0.0)
    q_dec = jnp.exp(lg[:, None] * (pos + 1.0))
    k_dec = jnp.exp(lg[:, None] * (chunk - 1.0 - pos))
    chunk_dec = jnp.exp(lg * chunk)
    qd = jnp.broadcast_to(q_dec[:, :, None], (H_RET, chunk, DV_RET))
    kd = jnp.broadcast_to(k_dec[:, :, None], (H_RET, chunk, DK_RET))
    cd = jnp.broadcast_to(chunk_dec[:, None, None], (H_RET, DK_RET, DV_RET))
    return dec, qd, kd, cd


def _ret_prompt(qb, kb, vb, rg, batch, n_chunks):
    t = qb.shape[0]
    chunk = RET_CHUNK
    rows = chunk * n_chunks
    steps = t // batch // rows
    row = lambda n: pl.BlockSpec((rows, n), lambda b, i: (b * steps + i, 0))
    const = lambda shp: pl.BlockSpec(shp, lambda b, i: (0,) * len(shp))
    tables = _ret_tables(chunk)
    return pl.pallas_call(
        functools.partial(_ret_prompt_kernel, chunk=chunk, n_chunks=n_chunks),
        grid=(batch, steps),
        in_specs=[row(D_RET_QK), row(D_RET_QK), row(D_RET_V), row(D_RET_V)] + [const(tb.shape) for tb in tables],
        out_specs=[row(D_RET_V), pl.BlockSpec((1, H_RET, DK_RET, DV_RET), lambda b, i: (b, 0, 0, 0))],
        out_shape=[jax.ShapeDtypeStruct((t, D_RET_V), BF16),
                   jax.ShapeDtypeStruct((batch, H_RET, DK_RET, DV_RET), F32)],
        scratch_shapes=[pltpu.VMEM((H_RET, DK_RET, DV_RET), F32)],
        name="ret_prompt",
        compiler_params=_cparams(("parallel", "arbitrary")),
    )(qb, kb, vb, rg, *tables)


def _ret_sample_kernel(q_ref, k_ref, v_ref, rg_ref, r0_ref, dec_ref, qd_ref, kd_ref, cd_ref, y_ref, rout_ref,
                       *, chunk, n_seq):
    for s in range(n_seq):
        rows = slice(s * chunk, (s + 1) * chunk)
        for h in range(H_RET):
            qk = slice(h * DK_RET, (h + 1) * DK_RET)
            vv = slice(h * DV_RET, (h + 1) * DV_RET)
            y, r_new = _ret_head(q_ref[rows, qk].astype(F32), k_ref[rows, qk].astype(F32),
                                 v_ref[rows, vv].astype(F32), rg_ref[rows, vv], r0_ref[s, h],
                                 dec_ref[h], qd_ref[h], kd_ref[h], cd_ref[h])
            y_ref[rows, vv] = y
            rout_ref[s, h] = r_new


def _ret_sample(qb, kb, vb, rg, r0, chunk, n_seq):
    t = qb.shape[0]
    rows = chunk * n_seq
    row = lambda n: pl.BlockSpec((rows, n), lambda i: (i, 0))
    const = lambda shp: pl.BlockSpec(shp, lambda i: (0,) * len(shp))
    st = pl.BlockSpec((n_seq, H_RET, DK_RET, DV_RET), lambda i: (i, 0, 0, 0))
    tables = _ret_tables(chunk)
    return pl.pallas_call(
        functools.partial(_ret_sample_kernel, chunk=chunk, n_seq=n_seq),
        grid=(t // rows,),
        in_specs=[row(D_RET_QK), row(D_RET_QK), row(D_RET_V), row(D_RET_V), st] + [const(tb.shape) for tb in tables],
        out_specs=[row(D_RET_V), st],
        out_shape=[jax.ShapeDtypeStruct((t, D_RET_V), BF16), jax.ShapeDtypeStruct(r0.shape, F32)],
        name="ret_sample",
        compiler_params=_cparams(("parallel",)),
    )(qb, kb, vb, rg, r0, *tables)


def _route(lg):
    lane = lax.broadcasted_iota(jnp.int32, lg.shape, 1)
    big = jnp.int32(1 << 20)
    is_grp = lane < N_GROUPS
    mg = jnp.max(jnp.where(is_grp, lg, -jnp.inf), axis=-1, keepdims=True)
    g_idx = jnp.min(jnp.where(is_grp & (lg == mg), lane, big), axis=-1, keepdims=True)
    g_w = 1.0 / jnp.sum(jnp.where(is_grp, jnp.exp(lg - mg), 0.0), axis=-1, keepdims=True)
    lo = _R_EXP0 + g_idx * EXPERTS_PER_GROUP
    sel = (lane >= lo) & (lane < lo + EXPERTS_PER_GROUP)
    v1 = jnp.max(jnp.where(sel, lg, -jnp.inf), axis=-1, keepdims=True)
    i1 = jnp.min(jnp.where(sel & (lg == v1), lane, big), axis=-1, keepdims=True)
    sel2 = sel & (lane != i1)
    v2 = jnp.max(jnp.where(sel2, lg, -jnp.inf), axis=-1, keepdims=True)
    i2 = jnp.min(jnp.where(sel2 & (lg == v2), lane, big), axis=-1, keepdims=True)
    e2 = jnp.exp(v2 - v1)
    w1 = g_w / (1.0 + e2)
    w2 = g_w * e2 / (1.0 + e2)
    return jnp.where(lane == i1, w1, 0.0) + jnp.where(lane == i2, w2, 0.0)


def _mix_kernel(yf_ref, yr_ref, ga_ref, gb_ref, x_ref, g1_ref, sc_ref, sh_ref, gain_ref,
                wf_ref, wr_ref, wo_ref, wrh_ref, wrl_ref, br_ref,
                x1_ref, h2_ref, cmb_ref, *, fox_transposed):
    if fox_transposed:
        pf = _dot_tn(yf_ref[0], wf_ref[...])
    else:
        pf = _dot(yf_ref[...], wf_ref[...])
    pr = _dot(yr_ref[...], wr_ref[...])
    mixed = jax.nn.sigmoid(ga_ref[...]) * pf + jax.nn.sigmoid(gb_ref[...]) * pr
    x1 = x_ref[...] + g1_ref[0] * _dot(mixed.astype(BF16), wo_ref[...])
    x1_ref[...] = x1
    y = x1 * lax.rsqrt(jnp.mean(x1 * x1, axis=-1, keepdims=True) + EPS) * gain_ref[...]
    h2 = y * (1.0 + sc_ref[0]) + sh_ref[0]
    h2_ref[...] = h2.astype(BF16)
    h_hi = h2.astype(BF16)
    h_lo = (h2 - h_hi.astype(F32)).astype(BF16)
    lg = _dot(h_hi, wrh_ref[...]) + (_dot(h_hi, wrl_ref[...]) + _dot(h_lo, wrh_ref[...])) + br_ref[...]
    cmb_ref[...] = _route(lg)


def _mix(yf, yr, ga, gb, x, g1, sc2, sh2, gain, wf, wr, wo, wrh, wrl, br, tm, tiles_per_batch, fox_transposed):
    t = x.shape[0]
    row = lambda n: pl.BlockSpec((tm, n), lambda i: (i, 0))
    const = lambda shp: pl.BlockSpec(shp, lambda i: (0,) * len(shp))
    if fox_transposed:
        yf_spec = pl.BlockSpec((1, D_FOX, tm), lambda i: (i // tiles_per_batch, 0, i % tiles_per_batch))
    else:
        yf_spec = row(D_FOX)
    ms = lambda m: _mod_spec(m, tiles_per_batch)
    return pl.pallas_call(
        functools.partial(_mix_kernel, fox_transposed=fox_transposed),
        grid=(t // tm,),
        in_specs=[yf_spec, row(D_RET_V), row(D_MODEL), row(D_MODEL), row(D_MODEL), ms(g1), ms(sc2), ms(sh2),
                  const((1, D_MODEL)), const((D_FOX, D_MODEL)), const((D_RET_V, D_MODEL)),
                  const((D_MODEL, D_MODEL)), const((D_MODEL, LANES)), const((D_MODEL, LANES)), const((1, LANES))],
        out_specs=[row(D_MODEL), row(D_MODEL), row(LANES)],
        out_shape=[jax.ShapeDtypeStruct((t, D_MODEL), F32), jax.ShapeDtypeStruct((t, D_MODEL), BF16),
                   jax.ShapeDtypeStruct((t, LANES), F32)],
        name="mix_route",
        compiler_params=_cparams(("parallel",)),
    )(yf, yr, ga, gb, x, g1, sc2, sh2, gain, wf, wr, wo, wrh, wrl, br)


def _moe_kernel(h_ref, cmb_ref, x1_ref, g2_ref, gain_ref, wg_ref, wu_ref, wd_ref, y_ref, acc_scr, hid_scr):
    g = pl.program_id(1)

    @pl.when(g == 0)
    def _():
        acc_scr[...] = jnp.zeros(acc_scr.shape, F32)

    h = h_ref[...]
    cmb = cmb_ref[...]
    lane = lax.broadcasted_iota(jnp.int32, cmb.shape, 1)
    for e in range(EXPERTS_PER_GROUP):
        a = _dot(h, wg_ref[0, e])
        u = _dot(h, wu_ref[0, e])
        w = jnp.sum(jnp.where(lane == _R_EXP0 + g * EXPERTS_PER_GROUP + e, cmb, 0.0), axis=-1, keepdims=True)
        hid_scr[:, e * D_EXPERT:(e + 1) * D_EXPERT] = (jax.nn.silu(a) * u * w).astype(BF16)
    acc_scr[...] += _dot(hid_scr[...], wd_ref[0])

    @pl.when(g == N_GROUPS - 1)
    def _():
        x2 = x1_ref[...] + g2_ref[0] * acc_scr[...]
        y_ref[...] = x2 * lax.rsqrt(jnp.mean(x2 * x2, axis=-1, keepdims=True) + EPS) * gain_ref[...]


def _moe(h2, cmb, x1, g2, gain, wg, wu, wd, tm, tiles_per_batch):
    t = h2.shape[0]
    row = lambda n: pl.BlockSpec((tm, n), lambda i, g: (i, 0))
    r = g2.shape[1]
    if tiles_per_batch is None:
        g2_spec = pl.BlockSpec((1, r, D_MODEL), lambda i, g: (i, 0, 0))
    else:
        g2_spec = pl.BlockSpec((1, r, D_MODEL), lambda i, g: (i // tiles_per_batch, 0, 0))
    hidden = EXPERTS_PER_GROUP * D_EXPERT
    return pl.pallas_call(
        _moe_kernel,
        grid=(t // tm, N_GROUPS),
        in_specs=[row(D_MODEL), row(LANES), row(D_MODEL), g2_spec,
                  pl.BlockSpec((1, D_MODEL), lambda i, g: (0, 0)),
                  pl.BlockSpec((1, EXPERTS_PER_GROUP, D_MODEL, D_EXPERT), lambda i, g: (g, 0, 0, 0)),
                  pl.BlockSpec((1, EXPERTS_PER_GROUP, D_MODEL, D_EXPERT), lambda i, g: (g, 0, 0, 0)),
                  pl.BlockSpec((1, hidden, D_MODEL), lambda i, g: (g, 0, 0))],
        out_specs=row(D_MODEL),
        out_shape=jax.ShapeDtypeStruct((t, D_MODEL), F32),
        scratch_shapes=[pltpu.VMEM((tm, D_MODEL), F32), pltpu.VMEM((tm, hidden), BF16)],
        name="moe_final",
        compiler_params=_cparams(("parallel", "arbitrary")),
    )(h2, cmb, x1, g2, gain, wg, wu, wd)


def _rope_tables(pos):
    half = DK_RET // 2
    inv = ROPE_BASE ** (-jnp.arange(half, dtype=F32) / half)
    ang = pos.astype(F32)[:, None] * inv[None, :]
    cos, sin = jnp.cos(ang), jnp.sin(ang)
    zero = jnp.zeros_like(sin)
    tile = lambda a, b: jnp.tile(jnp.concatenate([a, b], axis=-1), (1, H_RET))
    return tile(cos, cos), tile(-sin, zero), tile(zero, sin)


def _pick_tile(n, target):
    t = min(n, target)
    while n % t:
        t //= 2
    return t


def kernel(x_prompt, x_sample, cache_k, cache_v, cache_logf, state_ret, page_table, c_prompt, c_sample, w_ada, b_ada, norm_mix, norm_ffn, w_in, b_fgt, w_br_fox, w_br_ret, w_out, w_route_group, b_route_group, w_route_expert, b_route_expert, w_exp_gate, w_exp_up, w_exp_down, norm_final):
    depth = w_ada.shape[0]
    assert depth == 1
    bp, seq, d = x_prompt.shape
    bd, lq, _ = x_sample.shape
    n_pool, page_size = cache_k.shape[1], cache_k.shape[2]
    n_pages = page_table.shape[1]
    past_len = n_pages * page_size
    tp, ts = bp * seq, bd * lq

    wl = w_in[0]
    w_perm = jnp.concatenate(
        [wl[:, 0:1536], wl[:, 1544:5128], wl[:, 1536:1544], jnp.zeros((d, LANES - H_FOX), F32)], axis=1).astype(BF16)
    b_fgt_pad = jnp.pad(b_fgt[0], (0, LANES - H_FOX)).reshape(1, LANES)
    gain_mix, gain_ffn, gain_fin = norm_mix[0].reshape(1, d), norm_ffn[0].reshape(1, d), norm_final.reshape(1, d)
    wf, wr, wo = w_br_fox[0].astype(BF16), w_br_ret[0].astype(BF16), w_out[0].astype(BF16)
    w_rt = jnp.pad(jnp.concatenate([w_route_group[0], w_route_expert[0]], axis=1),
                   ((0, 0), (0, LANES - N_GROUPS - N_EXPERTS)))
    wrh = w_rt.astype(BF16)
    wrl = (w_rt - wrh.astype(F32)).astype(BF16)
    b_rt = jnp.pad(jnp.concatenate([b_route_group[0], b_route_expert[0]]), (0, LANES - N_GROUPS - N_EXPERTS))
    b_rt = b_rt.reshape(1, LANES)
    wg, wu = w_exp_gate[0].astype(BF16), w_exp_up[0].astype(BF16)
    wd = w_exp_down[0].astype(BF16).reshape(N_GROUPS, EXPERTS_PER_GROUP * D_EXPERT, d)

    n_c = bp + bd
    n_c_pad = -(-n_c // 8) * 8
    c_all = jnp.pad(jnp.concatenate([c_prompt, c_sample], axis=0), ((0, n_c_pad - n_c), (0, 0)))
    mod = _ada_mod(c_all, w_ada[0], b_ada[0])
    mods = [mod[:, i * d:(i + 1) * d] for i in range(6)]
    tm_p = _pick_tile(seq, 512)
    tm_s = _pick_tile(ts, 512)
    mod_p = [m[:bp].reshape(bp, 1, d) for m in mods]
    mod_s = [jnp.repeat(m[bp:n_c], lq, axis=0).reshape(ts // tm_s, tm_s, d) for m in mods]

    xp = x_prompt.reshape(tp, d)
    tpb = seq // tm_p
    cos_p, sa_p, sb_p = _rope_tables(jnp.arange(seq))
    (qa, ka, va, lf, qb, kb, vb, rg, ga, gb) = _in_proj(
        xp, mod_p[1], mod_p[0], gain_mix, w_perm, b_fgt_pad, cos_p, sa_p, sb_p, tm_p, tpb, tpb)
    logf_p = lf[:, :H_FOX].reshape(bp, seq, H_FOX)

    lf_rows = logf_p.transpose(0, 2, 1).reshape(bp * H_FOX, seq // LANES, LANES)
    c_hi, c_mid, c_lo = [c.reshape(bp, H_FOX, seq) for c in _cumsum_split(lf_rows)]

    tq = _pick_tile(seq, 512)
    tk = tq
    one = jnp.ones((bp, H_FOX, seq, 3), BF16)
    k_heads = ka.reshape(bp, seq, H_FOX, DH_FOX).transpose(0, 2, 1, 3).astype(BF16)
    k_aug = jnp.concatenate(
        [k_heads, one, -c_hi[..., None], -c_mid[..., None], -c_lo[..., None],
         jnp.zeros((bp, H_FOX, seq, LANES - DH_FOX - 6), BF16)], axis=-1)
    k_aug = k_aug.reshape(bp, H_FOX, seq // tk, tk, LANES)
    qt = qa.reshape(bp, seq, H_FOX, DH_FOX).transpose(0, 2, 3, 1)
    qt_aug = jnp.concatenate(
        [qt, c_hi[:, :, None, :], c_mid[:, :, None, :], c_lo[:, :, None, :], one.transpose(0, 1, 3, 2),
         jnp.zeros((bp, H_FOX, LANES - DH_FOX - 6, seq), BF16)], axis=2)
    vt = va.reshape(bp, seq // tk, tk, H_FOX, DH_FOX).transpose(0, 3, 1, 4, 2).astype(BF16)
    yf_t = _fox_prompt(qt_aug, k_aug, vt, tq, tk).reshape(bp, D_FOX, seq)

    n_chunks = _pick_tile(seq // RET_CHUNK, 4)
    yr_p, rstate_p = _ret_prompt(qb, kb, vb, rg, bp, n_chunks)

    x1_p, h2_p, cmb_p = _mix(yf_t, yr_p, ga, gb, xp, mod_p[2], mod_p[4], mod_p[3], gain_ffn,
                             wf, wr, wo, wrh, wrl, b_rt, tm_p, tpb, True)
    y_p = _moe(h2_p, cmb_p, x1_p, mod_p[5], gain_fin, wg, wu, wd, tm_p, tpb)

    xs = x_sample.reshape(ts, d)
    cos_s, sa_s, sb_s = [jnp.tile(tb, (tm_s // lq, 1)) for tb in _rope_tables(past_len + jnp.arange(lq))]
    (qa_s, ka_s, va_s, lf_s, qb_s, kb_s, vb_s, rg_s, ga_s, gb_s) = _in_proj(
        xs, mod_s[1], mod_s[0], gain_mix, w_perm, b_fgt_pad, cos_s, sa_s, sb_s, tm_s, None, 1)
    logf_s = lf_s[:, :H_FOX].reshape(bd, lq, H_FOX)

    q4 = qa_s.reshape(bd, lq, H_FOX, DH_FOX).transpose(0, 2, 1, 3)
    eye = jnp.eye(H_FOX, dtype=BF16)
    qt_bd = (q4[:, :, :, None, :] * eye[None, :, None, :, None]).reshape(bd, H_FOX * lq, D_FOX)
    pps = _pick_tile(n_pages, 8)
    yf_s = _fox_sample(page_table, cache_k[0].reshape(n_pool, page_size, D_FOX),
                       cache_v[0].reshape(n_pool, page_size, D_FOX), cache_logf[0],
                       qt_bd, ka_s.reshape(bd, lq, D_FOX).astype(BF16), va_s.reshape(bd, lq, D_FOX).astype(BF16),
                       logf_s, pps).reshape(ts, D_FOX)

    n_seq = _pick_tile(bd, 16)
    yr_s, rstate_s = _ret_sample(qb_s, kb_s, vb_s, rg_s, state_ret[0], lq, n_seq)

    x1_s, h2_s, cmb_s = _mix(yf_s, yr_s, ga_s, gb_s, xs, mod_s[2], mod_s[4], mod_s[3], gain_ffn,
                             wf, wr, wo, wrh, wrl, b_rt, tm_s, None, False)
    y_s = _moe(h2_s, cmb_s, x1_s, mod_s[5], gain_fin, wg, wu, wd, tm_s, None)

    return (y_p.reshape(bp, seq, d), y_s.reshape(bd, lq, d),
            ka.reshape(1, bp, seq, H_FOX, DH_FOX), va.reshape(1, bp, seq, H_FOX, DH_FOX),
            logf_p[None], rstate_p[None],
            ka_s.reshape(1, bd, lq, H_FOX, DH_FOX), va_s.reshape(1, bd, lq, H_FOX, DH_FOX),
            logf_s[None], rstate_s[None])
```

```python
import functools

import numpy as np
import jax
import jax.numpy as jnp
from jax import lax
from jax.experimental import pallas as pl
from jax.experimental.pallas import tpu as pltpu

F32 = jnp.float32
BF16 = jnp.bfloat16

D_MODEL = 1024
H_FOX, DH_FOX = 8, 64
D_FOX = H_FOX * DH_FOX
H_RET, DK_RET, DV_RET = 4, 64, 128
D_RET_QK = H_RET * DK_RET
D_RET_V = H_RET * DV_RET
RET_CHUNK = 128
ROPE_BASE = 10000.0
N_GROUPS, EXPERTS_PER_GROUP, D_EXPERT = 4, 8, 256
N_EXPERTS = N_GROUPS * EXPERTS_PER_GROUP
EPS = 1e-6
LOG2E = 1.4426950408889634
NEG_BIG = -1e30
LANES = 128
KEY_STRIP = 256

_C_QA, _C_KA, _C_VA = 0, 512, 1024
_C_QB, _C_KB, _C_VB, _C_RG, _C_GA, _C_GB, _C_FA, _C_END = 1536, 1792, 2048, 2560, 3072, 4096, 5120, 5248
_R_EXP0 = N_GROUPS

VMEM_LIMIT = 56 * 1024 * 1024


def _cparams(sem, vmem=VMEM_LIMIT, flags=None):
    return pltpu.CompilerParams(dimension_semantics=sem, vmem_limit_bytes=vmem, flags=flags)


def _split3(x):
    hi = x.astype(BF16)
    r1 = x - hi.astype(F32)
    mid = r1.astype(BF16)
    lo = (r1 - mid.astype(F32)).astype(BF16)
    return hi, mid, lo


def _dot(a, b):
    return jnp.dot(a, b, preferred_element_type=F32)


def _dot_nt(a, b):
    return lax.dot_general(a, b, (((1,), (1,)), ((), ())), preferred_element_type=F32)


def _dot_tn(a, b):
    return lax.dot_general(a, b, (((0,), (0,)), ((), ())), preferred_element_type=F32)


def _dot3_rhs_exact(x, m):
    return sum(_dot(p, m) for p in _split3(x))


def _dot3_lhs_exact(m, x):
    return sum(_dot(m, p) for p in _split3(x))


def _ada_kernel(c_ref, w_ref, b_ref, o_ref):
    a = jax.nn.silu(c_ref[...]).astype(BF16)
    o_ref[...] = _dot(a, w_ref[...].astype(BF16)) + b_ref[...]


def _ada_mod(c_all, w_ada, b_ada):
    rows = c_all.shape[0]
    n = w_ada.shape[1]
    tn = 1536
    return pl.pallas_call(
        _ada_kernel,
        grid=(n // tn,),
        in_specs=[pl.BlockSpec((rows, D_MODEL), lambda j: (0, 0)),
                  pl.BlockSpec((D_MODEL, tn), lambda j: (0, j)),
                  pl.BlockSpec((1, tn), lambda j: (0, j))],
        out_specs=pl.BlockSpec((rows, tn), lambda j: (0, j)),
        out_shape=jax.ShapeDtypeStruct((rows, n), F32),
        name="ada_mod",
        compiler_params=_cparams(("parallel",)),
    )(c_all, w_ada, b_ada.reshape(1, n))


def _inproj_kernel(x_ref, sc_ref, sh_ref, g_ref, w_ref, bf_ref, cos_ref, sa_ref, sb_ref, *out_refs, head_major):
    x = x_ref[...]
    tm = x.shape[0]
    y = x * lax.rsqrt(jnp.mean(x * x, axis=-1, keepdims=True) + EPS) * g_ref[...]
    h = (y * (1.0 + sc_ref[0]) + sh_ref[0]).astype(BF16)

    def proj(lo, hi):
        return _dot(h, w_ref[:, lo:hi])

    cos, sa, sb = cos_ref[...], sa_ref[...], sb_ref[...]

    def rope(t):
        return t * cos + pltpu.roll(t, D_RET_QK - DK_RET // 2, 1) * sa + pltpu.roll(t, DK_RET // 2, 1) * sb

    q = proj(_C_QA, _C_KA) * (DH_FOX ** -0.5 * LOG2E)
    k = proj(_C_KA, _C_VA)
    v = proj(_C_VA, _C_QB)
    if head_major:
        qt_ref, kw_ref, kt_ref, vt_ref, vc_ref = out_refs[:5]
        rest, t_scr = out_refs[5:-1], out_refs[-1]

        def transposed(a):
            t_scr[...] = a
            return t_scr[...].T

        qt_ref[0] = transposed(q).astype(BF16).reshape(H_FOX, DH_FOX, tm)
        kt_ref[0] = transposed(k).reshape(H_FOX, DH_FOX, tm)
        vt = transposed(v)
        vt_ref[0] = vt.reshape(H_FOX, DH_FOX, tm)
        vtb = vt.astype(BF16)
        lane = lax.broadcasted_iota(jnp.int32, (tm, LANES), 1)
        ones = jnp.where((lane >= DH_FOX) & (lane < DH_FOX + 3), 1.0, 0.0)
        for hh in range(H_FOX):
            pair = k[:, (hh // 2) * LANES:(hh // 2 + 1) * LANES]
            if hh % 2:
                pair = pltpu.roll(pair, DH_FOX, 1)
            kw_ref[0, hh] = jnp.where(lane < DH_FOX, pair, ones).astype(BF16)
            for c in range(tm // KEY_STRIP):
                vc_ref[0, hh, c] = vtb[hh * DH_FOX:(hh + 1) * DH_FOX, c * KEY_STRIP:(c + 1) * KEY_STRIP]
    else:
        q_ref, k_ref, v_ref = out_refs[:3]
        rest = out_refs[3:]
        q_ref[...] = q.astype(BF16)
        k_ref[...] = k
        v_ref[...] = v
    lf_ref, qb_ref, kb_ref, vb_ref, rg_ref, ga_ref, gb_ref = rest
    qb_ref[...] = rope(proj(_C_QB, _C_KB)).astype(BF16)
    kb_ref[...] = (rope(proj(_C_KB, _C_VB)) * DK_RET ** -0.5).astype(BF16)
    vb_ref[...] = proj(_C_VB, _C_RG).astype(BF16)
    rg_ref[...] = proj(_C_RG, _C_GA)
    ga_ref[...] = proj(_C_GA, _C_GB)
    gb_ref[...] = proj(_C_GB, _C_FA)
    lf_ref[...] = jax.nn.log_sigmoid(proj(_C_FA, _C_END) + bf_ref[...])


def _mod_spec(mod, tiles_per_batch):
    r = mod.shape[1]
    if tiles_per_batch is None:
        return pl.BlockSpec((1, r, D_MODEL), lambda i: (i, 0, 0))
    return pl.BlockSpec((1, r, D_MODEL), lambda i: (i // tiles_per_batch, 0, 0))


def _in_proj(x, sc, sh, gain, w_perm, b_fgt_pad, cos, sa, sb, tm, tiles_per_batch, rope_tiles, head_major):
    t = x.shape[0]
    row = lambda n: pl.BlockSpec((tm, n), lambda i: (i, 0))
    const = lambda shp: pl.BlockSpec(shp, lambda i: (0,) * len(shp), pipeline_mode=pl.Buffered(1))
    rope_spec = pl.BlockSpec((tm, D_RET_QK), lambda i: (i % rope_tiles, 0))
    outs = [(LANES, F32), (D_RET_QK, BF16), (D_RET_QK, BF16), (D_RET_V, BF16), (D_RET_V, F32),
            (D_MODEL, F32), (D_MODEL, F32)]
    if head_major:
        tpb = tiles_per_batch
        nb, seq = t // (tpb * tm), tpb * tm
        tspec = pl.BlockSpec((1, H_FOX, DH_FOX, tm), lambda i: (i // tpb, 0, 0, i % tpb))
        head_specs = [tspec, pl.BlockSpec((1, H_FOX, tm, LANES), lambda i: (i // tpb, 0, i % tpb, 0)), tspec, tspec,
                      pl.BlockSpec((1, H_FOX, tm // KEY_STRIP, DH_FOX, KEY_STRIP),
                                   lambda i: (i // tpb, 0, i % tpb, 0, 0))]
        head_shapes = [jax.ShapeDtypeStruct((nb, H_FOX, DH_FOX, seq), BF16),
                       jax.ShapeDtypeStruct((nb, H_FOX, seq, LANES), BF16),
                       jax.ShapeDtypeStruct((nb, H_FOX, DH_FOX, seq), F32),
                       jax.ShapeDtypeStruct((nb, H_FOX, DH_FOX, seq), F32),
                       jax.ShapeDtypeStruct((nb, H_FOX, seq // KEY_STRIP, DH_FOX, KEY_STRIP), BF16)]
    else:
        head_specs = [row(D_FOX)] * 3
        head_shapes = [jax.ShapeDtypeStruct((t, D_FOX), dt) for dt in (BF16, F32, F32)]
    return pl.pallas_call(
        functools.partial(_inproj_kernel, head_major=head_major),
        grid=(t // tm,),
        in_specs=[row(D_MODEL), _mod_spec(sc, tiles_per_batch), _mod_spec(sh, tiles_per_batch),
                  const((1, D_MODEL)), const((D_MODEL, _C_END)), const((1, LANES)),
                  rope_spec, rope_spec, rope_spec],
        out_specs=head_specs + [row(n) for n, _ in outs],
        out_shape=head_shapes + [jax.ShapeDtypeStruct((t, n), dt) for n, dt in outs],
        scratch_shapes=[pltpu.VMEM((tm, D_FOX), F32)] if head_major else [],
        name="in_proj",
        compiler_params=_cparams(("parallel",)),
    )(x, sc, sh, gain, w_perm, b_fgt_pad, cos, sa, sb)


def _cumsum_kernel(x_ref, hi_ref, mid_ref, lo_ref):
    x = x_ref[0]
    r = x.shape[0]
    a = lax.broadcasted_iota(jnp.int32, (LANES, LANES), 0)
    b = lax.broadcasted_iota(jnp.int32, (LANES, LANES), 1)
    incl = (a <= b).astype(BF16)
    y = _dot3_rhs_exact(x, incl)
    tot = jnp.broadcast_to(y[:, LANES - 1:LANES], (r, LANES))
    ra = lax.broadcasted_iota(jnp.int32, (r, r), 0)
    rb = lax.broadcasted_iota(jnp.int32, (r, r), 1)
    strict = (rb < ra).astype(BF16)
    c = (y + _dot3_lhs_exact(strict, tot)) * LOG2E
    hi, mid, lo = _split3(c)
    hi_ref[0] = hi
    mid_ref[0] = mid
    lo_ref[0] = lo


def _cumsum_split(lf_rows):
    n, r, _ = lf_rows.shape
    spec = pl.BlockSpec((1, r, LANES), lambda i: (i, 0, 0))
    return pl.pallas_call(
        _cumsum_kernel,
        grid=(n,),
        in_specs=[spec],
        out_specs=[spec] * 3,
        out_shape=[jax.ShapeDtypeStruct((n, r, LANES), BF16)] * 3,
        name="logf_cumsum",
        compiler_params=_cparams(("parallel",)),
    )(lf_rows)


_C_ROWS = 16


def _fox_prompt_kernel(q_ref, ca_ref, k_ref, v_ref, o_ref, m_scr, l_scr, acc_scr, *, tq, ks):
    i = pl.program_id(2)
    n_heads = q_ref.shape[1]
    per = tq // ks
    qts = [jnp.concatenate([q_ref[0, g], ca_ref[0, g], jnp.zeros((LANES - DH_FOX - _C_ROWS, tq), BF16)], axis=0)
           for g in range(n_heads)]

    m_scr[...] = jnp.full(m_scr.shape, NEG_BIG, F32)
    l_scr[...] = jnp.zeros(l_scr.shape, F32)
    acc_scr[...] = jnp.zeros(acc_scr.shape, F32)

    def strip(c, masked):
        r0 = pl.multiple_of(c * ks, ks)
        scores = [_dot(k_ref[0, g, pl.ds(r0, ks), :], qts[g]) for g in range(n_heads)]
        for g in range(n_heads):
            s = scores[g]
            if masked:
                kpos = c * ks + lax.broadcasted_iota(jnp.int32, (ks, tq), 0)
                qpos = i * tq + lax.broadcasted_iota(jnp.int32, (ks, tq), 1)
                s = jnp.where(kpos <= qpos, s, NEG_BIG)
            m = m_scr[g]
            m_new = jnp.maximum(m, jnp.max(s, axis=0, keepdims=True))
            alpha = jnp.exp2(m - m_new)
            p = jnp.exp2(s - m_new)
            l_scr[g] = alpha * l_scr[g] + jnp.sum(p, axis=0, keepdims=True)
            acc_scr[g] = alpha * acc_scr[g] + _dot(v_ref[0, g, c], p.astype(BF16))
            m_scr[g] = m_new

    def block(j, carry):
        strip(j, False)
        return carry

    lax.fori_loop(0, i * per, block, 0)
    for u in range(per):
        strip(i * per + u, True)
    for g in range(n_heads):
        o_ref[0, g] = (acc_scr[g] / l_scr[g]).astype(BF16)


def _fox_prompt(qt, c_rows, k_aug, vt_chunks, tq, heads_per_step):
    b, h, _, s = qt.shape
    g = heads_per_step
    ks = vt_chunks.shape[-1]
    qmap = lambda bi, hi, i: (bi, hi, 0, i)
    resident = lambda shp: pl.BlockSpec(shp, lambda bi, hi, i: (bi, hi) + (0,) * (len(shp) - 2),
                                        pipeline_mode=pl.Buffered(1))
    return pl.pallas_call(
        functools.partial(_fox_prompt_kernel, tq=tq, ks=ks),
        grid=(b, h // g, s // tq),
        in_specs=[pl.BlockSpec((1, g, DH_FOX, tq), qmap),
                  pl.BlockSpec((1, g, _C_ROWS, tq), qmap),
                  resident((1, g, s, LANES)),
                  resident((1, g, s // ks, DH_FOX, ks))],
        out_specs=pl.BlockSpec((1, g, DH_FOX, tq), qmap),
        out_shape=jax.ShapeDtypeStruct((b, h, DH_FOX, s), BF16),
        scratch_shapes=[pltpu.VMEM((g, 1, tq), F32), pltpu.VMEM((g, 1, tq), F32),
                        pltpu.VMEM((g, DH_FOX, tq), F32)],
        name="fox_prompt",
        compiler_params=_cparams(("parallel", "parallel", "arbitrary")),
    )(qt, c_rows, k_aug, vt_chunks)


def _fox_sample_kernel(pt_ref, *refs, pps, n_steps, lq, ps):
    k_refs = refs[:pps]
    v_refs = refs[pps:2 * pps]
    f_refs = refs[2 * pps:3 * pps]
    qt_ref, kn_ref, vn_ref, fn_ref, o_ref, m_scr, l_scr, acc_scr, car_scr = refs[3 * pps:]
    del pt_ref
    hq = H_FOX * lq
    step = pl.program_id(1)

    row_h = lax.broadcasted_iota(jnp.int32, (hq, H_FOX), 0) // lq
    col_h = lax.broadcasted_iota(jnp.int32, (hq, H_FOX), 1)
    rep = (row_h == col_h).astype(BF16)

    @pl.when(step == 0)
    def _():
        m_scr[...] = jnp.full(m_scr.shape, NEG_BIG, F32)
        l_scr[...] = jnp.zeros(l_scr.shape, F32)
        acc_scr[...] = jnp.zeros(acc_scr.shape, F32)
        car_scr[...] = jnp.zeros(car_scr.shape, F32)

    qt = qt_ref[0]

    def update(scores, pv_fn):
        mx = scores[0]
        for s in scores[1:]:
            mx = jnp.maximum(mx, s)
        m = m_scr[...]
        m_new = jnp.maximum(m, jnp.max(mx, axis=-1, keepdims=True))
        alpha = jnp.exp2(m - m_new)
        ps = [jnp.exp2(s - m_new) for s in scores]
        tot = ps[0]
        for p in ps[1:]:
            tot = tot + p
        l_scr[...] = alpha * l_scr[...] + jnp.sum(tot, axis=-1, keepdims=True)
        acc_scr[...] = alpha * acc_scr[...] + pv_fn([p.astype(BF16) for p in ps])
        m_scr[...] = m_new

    car = car_scr[...]
    biases = []
    for j in range(pps):
        f = f_refs[j][...]
        biases.append(f[:, :ps] + car)
        car = car + f[:, ps:]
    car_scr[...] = car
    bias_rows = _dot3_lhs_exact(rep, jnp.concatenate(biases, axis=1))
    scores = [_dot(qt, k_refs[j][0].astype(BF16)) + bias_rows[:, j * ps:(j + 1) * ps] for j in range(pps)]

    def pv_pages(ps_bf):
        pv = _dot_nt(ps_bf[0], v_refs[0][0].astype(BF16))
        for j in range(1, pps):
            pv = pv + _dot_nt(ps_bf[j], v_refs[j][0].astype(BF16))
        return pv

    update(scores, pv_pages)

    @pl.when(step == n_steps - 1)
    def _():
        na = lax.broadcasted_iota(jnp.int32, (lq, lq), 0)
        nb = lax.broadcasted_iota(jnp.int32, (lq, lq), 1)
        c_new = _dot3_rhs_exact(fn_ref[0], (na <= nb).astype(BF16)) * LOG2E
        cj = _dot3_lhs_exact(rep, c_new)
        qi = lax.broadcasted_iota(jnp.int32, (hq, lq), 0) % lq
        kj = lax.broadcasted_iota(jnp.int32, (hq, lq), 1)
        s = jnp.where(kj <= qi, _dot_nt(qt, kn_ref[0]) - cj, NEG_BIG)
        update([s], lambda ps_bf: _dot(ps_bf[0], vn_ref[0]))
        acc = acc_scr[...] / l_scr[...]
        rh = lax.broadcasted_iota(jnp.int32, (hq, D_FOX), 0) // lq
        ch = lax.broadcasted_iota(jnp.int32, (hq, D_FOX), 1) // DH_FOX
        acc = jnp.where(rh == ch, acc, 0.0)
        out = acc[0:lq]
        for h in range(1, H_FOX):
            out = out + acc[h * lq:(h + 1) * lq]
        o_ref[0] = out.astype(BF16)


def _fox_sample(page_table, kt_pages, vt_pages, f_pages, qt_bd, k_new, v_new, f_new_t, pps):
    bd, n_pages = page_table.shape
    lq = k_new.shape[1]
    hq = H_FOX * lq
    n_steps = n_pages // pps
    ps = kt_pages.shape[2]

    def page_map(j):
        return lambda b, p, pt: (pt[b, n_pages - 1 - (p * pps + j)], 0, 0)

    def page_map2(j):
        return lambda b, p, pt: (pt[b, n_pages - 1 - (p * pps + j)], 0)

    per_b = lambda shp: pl.BlockSpec((1,) + shp, lambda b, p, pt: (b, 0, 0))
    in_specs = ([pl.BlockSpec((1, D_FOX, ps), page_map(j)) for j in range(pps)]
                + [pl.BlockSpec((1, D_FOX, ps), page_map(j)) for j in range(pps)]
                + [pl.BlockSpec((H_FOX, 2 * ps), page_map2(j)) for j in range(pps)]
                + [per_b((hq, D_FOX)), per_b((lq, D_FOX)), per_b((lq, D_FOX)), per_b((H_FOX, lq))])
    grid_spec = pltpu.PrefetchScalarGridSpec(
        num_scalar_prefetch=1,
        grid=(bd, n_steps),
        in_specs=in_specs,
        out_specs=per_b((lq, D_FOX)),
        scratch_shapes=[pltpu.VMEM((hq, 1), F32), pltpu.VMEM((hq, 1), F32),
                        pltpu.VMEM((hq, D_FOX), F32), pltpu.VMEM((H_FOX, ps), F32)],
    )
    return pl.pallas_call(
        functools.partial(_fox_sample_kernel, pps=pps, n_steps=n_steps, lq=lq, ps=ps),
        grid_spec=grid_spec,
        out_shape=jax.ShapeDtypeStruct((bd, lq, D_FOX), BF16),
        name="fox_sample",
        compiler_params=_cparams(("parallel", "arbitrary")),
    )(page_table, *([kt_pages] * pps), *([vt_pages] * pps), *([f_pages] * pps), qt_bd, k_new, v_new, f_new_t)


def _page_suffix_kernel(x_ref, o_ref):
    x = x_ref[...]
    ps = x.shape[1]
    ja = lax.broadcasted_iota(jnp.int32, (ps, ps), 0)
    jb = lax.broadcasted_iota(jnp.int32, (ps, ps), 1)
    o_ref[:, :ps] = _dot3_rhs_exact(x, (ja > jb).astype(BF16)) * LOG2E
    o_ref[:, ps:] = _dot3_rhs_exact(x, jnp.ones((ps, ps), BF16)) * LOG2E


def _page_suffix(lf_rows):
    n, ps = lf_rows.shape
    tr = _pick_tile(n, 4096)
    return pl.pallas_call(
        _page_suffix_kernel,
        grid=(n // tr,),
        in_specs=[pl.BlockSpec((tr, ps), lambda i: (i, 0))],
        out_specs=pl.BlockSpec((tr, 2 * ps), lambda i: (i, 0)),
        out_shape=jax.ShapeDtypeStruct((n, 2 * ps), F32),
        name="page_suffix",
        compiler_params=_cparams(("parallel",)),
    )(lf_rows)


def _ret_head(q, k, v, rg, r, dec, qd, kd, cd):
    mm = q.dtype
    inner = _dot_nt(q, k) * dec
    o = _dot(inner.astype(mm), v) + qd * _dot(q, r.astype(mm))
    r_new = cd * r + _dot_tn((k.astype(F32) * kd).astype(mm), v)
    mu = jnp.mean(o, axis=-1, keepdims=True)
    var = jnp.mean(jnp.square(o - mu), axis=-1, keepdims=True)
    y = (o - mu) * lax.rsqrt(var + EPS)
    return (y * jax.nn.silu(rg)).astype(BF16), r_new


def _ret_prompt_kernel(q_ref, k_ref, v_ref, rg_ref, dec_ref, qd_ref, kd_ref, cd_ref, y_ref, rout_ref, r_scr,
                       *, chunk, n_chunks):
    step = pl.program_id(1)

    @pl.when(step == 0)
    def _():
        r_scr[...] = jnp.zeros(r_scr.shape, F32)

    for c in range(n_chunks):
        rows = slice(c * chunk, (c + 1) * chunk)
        for h in range(H_RET):
            qk = slice(h * DK_RET, (h + 1) * DK_RET)
            vv = slice(h * DV_RET, (h + 1) * DV_RET)
            y, r_new = _ret_head(q_ref[rows, qk], k_ref[rows, qk], v_ref[rows, vv], rg_ref[rows, vv], r_scr[h],
                                 dec_ref[h], qd_ref[h], kd_ref[h], cd_ref[h])
            y_ref[rows, vv] = y
            r_scr[h] = r_new

    @pl.when(step == pl.num_programs(1) - 1)
    def _():
        rout_ref[0] = r_scr[...]


def _ret_tables(chunk):
    lg = jnp.log1p(-jnp.exp2(-5.0 - jnp.arange(H_RET, dtype=F32)))
    pos = jnp.arange(chunk, dtype=F32)
    diff = pos[:, None] - pos[None, :]
    dec = jnp.where(diff >= 0, jnp.exp(lg[:, None, None] * jnp.maximum(diff, 0.0)), 0.0)
    q_dec = jnp.exp(lg[:, None] * (pos + 1.0))
    k_dec = jnp.exp(lg[:, None] * (chunk - 1.0 - pos))
    chunk_dec = jnp.exp(lg * chunk)
    qd = jnp.broadcast_to(q_dec[:, :, None], (H_RET, chunk, DV_RET))
    kd = jnp.broadcast_to(k_dec[:, :, None], (H_RET, chunk, DK_RET))
    cd = jnp.broadcast_to(chunk_dec[:, None, None], (H_RET, DK_RET, DV_RET))
    return dec, qd, kd, cd


def _ret_prompt(qb, kb, vb, rg, batch, n_chunks):
    t = qb.shape[0]
    chunk = RET_CHUNK
    rows = chunk * n_chunks
    steps = t // batch // rows
    row = lambda n: pl.BlockSpec((rows, n), lambda b, i: (b * steps + i, 0))
    const = lambda shp: pl.BlockSpec(shp, lambda b, i: (0,) * len(shp))
    tables = _ret_tables(chunk)
    return pl.pallas_call(
        functools.partial(_ret_prompt_kernel, chunk=chunk, n_chunks=n_chunks),
        grid=(batch, steps),
        in_specs=[row(D_RET_QK), row(D_RET_QK), row(D_RET_V), row(D_RET_V)] + [const(tb.shape) for tb in tables],
        out_specs=[row(D_RET_V), pl.BlockSpec((1, H_RET, DK_RET, DV_RET), lambda b, i: (b, 0, 0, 0))],
        out_shape=[jax.ShapeDtypeStruct((t, D_RET_V), BF16),
                   jax.ShapeDtypeStruct((batch, H_RET, DK_RET, DV_RET), F32)],
        scratch_shapes=[pltpu.VMEM((H_RET, DK_RET, DV_RET), F32)],
        name="ret_prompt",
        compiler_params=_cparams(("parallel", "arbitrary")),
    )(qb, kb, vb, rg, *tables)


def _ret_sample_kernel(q_ref, k_ref, v_ref, rg_ref, r0_ref, dec_ref, qd_ref, kd_ref, cd_ref, y_ref, rout_ref,
                       *, chunk, n_seq):
    for s in range(n_seq):
        rows = slice(s * chunk, (s + 1) * chunk)
        for h in range(H_RET):
            qk = slice(h * DK_RET, (h + 1) * DK_RET)
            vv = slice(h * DV_RET, (h + 1) * DV_RET)
            y, r_new = _ret_head(q_ref[rows, qk].astype(F32), k_ref[rows, qk].astype(F32),
                                 v_ref[rows, vv].astype(F32), rg_ref[rows, vv], r0_ref[s, h],
                                 dec_ref[h], qd_ref[h], kd_ref[h], cd_ref[h])
            y_ref[rows, vv] = y
            rout_ref[s, h] = r_new


def _ret_sample(qb, kb, vb, rg, r0, chunk, n_seq):
    t = qb.shape[0]
    rows = chunk * n_seq
    row = lambda n: pl.BlockSpec((rows, n), lambda i: (i, 0))
    const = lambda shp: pl.BlockSpec(shp, lambda i: (0,) * len(shp))
    st = pl.BlockSpec((n_seq, H_RET, DK_RET, DV_RET), lambda i: (i, 0, 0, 0))
    tables = _ret_tables(chunk)
    return pl.pallas_call(
        functools.partial(_ret_sample_kernel, chunk=chunk, n_seq=n_seq),
        grid=(t // rows,),
        in_specs=[row(D_RET_QK), row(D_RET_QK), row(D_RET_V), row(D_RET_V), st] + [const(tb.shape) for tb in tables],
        out_specs=[row(D_RET_V), st],
        out_shape=[jax.ShapeDtypeStruct((t, D_RET_V), BF16), jax.ShapeDtypeStruct(r0.shape, F32)],
        name="ret_sample",
        compiler_params=_cparams(("parallel",)),
    )(qb, kb, vb, rg, r0, *tables)


def _route(lg):
    lane = lax.broadcasted_iota(jnp.int32, lg.shape, 1)
    big = jnp.int32(1 << 20)
    is_grp = lane < N_GROUPS
    mg = jnp.max(jnp.where(is_grp, lg, -jnp.inf), axis=-1, keepdims=True)
    g_idx = jnp.min(jnp.where(is_grp & (lg == mg), lane, big), axis=-1, keepdims=True)
    g_w = 1.0 / jnp.sum(jnp.where(is_grp, jnp.exp(lg - mg), 0.0), axis=-1, keepdims=True)
    lo = _R_EXP0 + g_idx * EXPERTS_PER_GROUP
    sel = (lane >= lo) & (lane < lo + EXPERTS_PER_GROUP)
    v1 = jnp.max(jnp.where(sel, lg, -jnp.inf), axis=-1, keepdims=True)
    i1 = jnp.min(jnp.where(sel & (lg == v1), lane, big), axis=-1, keepdims=True)
    sel2 = sel & (lane != i1)
    v2 = jnp.max(jnp.where(sel2, lg, -jnp.inf), axis=-1, keepdims=True)
    i2 = jnp.min(jnp.where(sel2 & (lg == v2), lane, big), axis=-1, keepdims=True)
    e2 = jnp.exp(v2 - v1)
    w1 = g_w / (1.0 + e2)
    w2 = g_w * e2 / (1.0 + e2)
    return jnp.where(lane == i1, w1, 0.0) + jnp.where(lane == i2, w2, 0.0)


def _mix_kernel(yf_ref, yr_ref, ga_ref, gb_ref, x_ref, g1_ref, sc_ref, sh_ref, gain_ref,
                wf_ref, wr_ref, wo_ref, wrh_ref, wrl_ref, br_ref,
                x1_ref, h2_ref, cmb_ref, *, fox_transposed):
    if fox_transposed:
        pf = _dot_tn(yf_ref[0], wf_ref[...])
    else:
        pf = _dot(yf_ref[...], wf_ref[...])
    pr = _dot(yr_ref[...], wr_ref[...])
    mixed = jax.nn.sigmoid(ga_ref[...]) * pf + jax.nn.sigmoid(gb_ref[...]) * pr
    x1 = x_ref[...] + g1_ref[0] * _dot(mixed.astype(BF16), wo_ref[...])
    x1_ref[...] = x1
    y = x1 * lax.rsqrt(jnp.mean(x1 * x1, axis=-1, keepdims=True) + EPS) * gain_ref[...]
    h2 = y * (1.0 + sc_ref[0]) + sh_ref[0]
    h2_ref[...] = h2.astype(BF16)
    h_hi = h2.astype(BF16)
    h_lo = (h2 - h_hi.astype(F32)).astype(BF16)
    lg = _dot(h_hi, wrh_ref[...]) + (_dot(h_hi, wrl_ref[...]) + _dot(h_lo, wrh_ref[...])) + br_ref[...]
    cmb_ref[...] = _route(lg)


def _mix(yf, yr, ga, gb, x, g1, sc2, sh2, gain, wf, wr, wo, wrh, wrl, br, tm, tiles_per_batch, fox_transposed):
    t = x.shape[0]
    row = lambda n: pl.BlockSpec((tm, n), lambda i: (i, 0))
    const = lambda shp: pl.BlockSpec(shp, lambda i: (0,) * len(shp))
    if fox_transposed:
        yf_spec = pl.BlockSpec((1, D_FOX, tm), lambda i: (i // tiles_per_batch, 0, i % tiles_per_batch))
    else:
        yf_spec = row(D_FOX)
    ms = lambda m: _mod_spec(m, tiles_per_batch)
    return pl.pallas_call(
        functools.partial(_mix_kernel, fox_transposed=fox_transposed),
        grid=(t // tm,),
        in_specs=[yf_spec, row(D_RET_V), row(D_MODEL), row(D_MODEL), row(D_MODEL), ms(g1), ms(sc2), ms(sh2),
                  const((1, D_MODEL)), const((D_FOX, D_MODEL)), const((D_RET_V, D_MODEL)),
                  const((D_MODEL, D_MODEL)), const((D_MODEL, LANES)), const((D_MODEL, LANES)), const((1, LANES))],
        out_specs=[row(D_MODEL), row(D_MODEL), row(LANES)],
        out_shape=[jax.ShapeDtypeStruct((t, D_MODEL), F32), jax.ShapeDtypeStruct((t, D_MODEL), BF16),
                   jax.ShapeDtypeStruct((t, LANES), F32)],
        name="mix_route",
        compiler_params=_cparams(("parallel",)),
    )(yf, yr, ga, gb, x, g1, sc2, sh2, gain, wf, wr, wo, wrh, wrl, br)


def _moe_kernel(h_ref, cmb_ref, x1_ref, g2_ref, gain_ref, wg_ref, wu_ref, wd_ref, y_ref, acc_scr, hid_scr):
    g = pl.program_id(1)

    @pl.when(g == 0)
    def _():
        acc_scr[...] = jnp.zeros(acc_scr.shape, F32)

    h = h_ref[...]
    cmb = cmb_ref[...]
    lane = lax.broadcasted_iota(jnp.int32, cmb.shape, 1)
    for e in range(EXPERTS_PER_GROUP):
        a = _dot(h, wg_ref[0, e])
        u = _dot(h, wu_ref[0, e])
        w = jnp.sum(jnp.where(lane == _R_EXP0 + g * EXPERTS_PER_GROUP + e, cmb, 0.0), axis=-1, keepdims=True)
        hid_scr[:, e * D_EXPERT:(e + 1) * D_EXPERT] = (jax.nn.silu(a) * u * w).astype(BF16)
    acc_scr[...] += _dot(hid_scr[...], wd_ref[0])

    @pl.when(g == N_GROUPS - 1)
    def _():
        x2 = x1_ref[...] + g2_ref[0] * acc_scr[...]
        y_ref[...] = x2 * lax.rsqrt(jnp.mean(x2 * x2, axis=-1, keepdims=True) + EPS) * gain_ref[...]


def _moe(h2, cmb, x1, g2, gain, wg, wu, wd, tm, tiles_per_batch):
    t = h2.shape[0]
    row = lambda n: pl.BlockSpec((tm, n), lambda i, g: (i, 0))
    r = g2.shape[1]
    if tiles_per_batch is None:
        g2_spec = pl.BlockSpec((1, r, D_MODEL), lambda i, g: (i, 0, 0))
    else:
        g2_spec = pl.BlockSpec((1, r, D_MODEL), lambda i, g: (i // tiles_per_batch, 0, 0))
    hidden = EXPERTS_PER_GROUP * D_EXPERT
    return pl.pallas_call(
        _moe_kernel,
        grid=(t // tm, N_GROUPS),
        in_specs=[row(D_MODEL), row(LANES), row(D_MODEL), g2_spec,
                  pl.BlockSpec((1, D_MODEL), lambda i, g: (0, 0)),
                  pl.BlockSpec((1, EXPERTS_PER_GROUP, D_MODEL, D_EXPERT), lambda i, g: (g, 0, 0, 0)),
                  pl.BlockSpec((1, EXPERTS_PER_GROUP, D_MODEL, D_EXPERT), lambda i, g: (g, 0, 0, 0)),
                  pl.BlockSpec((1, hidden, D_MODEL), lambda i, g: (g, 0, 0))],
        out_specs=row(D_MODEL),
        out_shape=jax.ShapeDtypeStruct((t, D_MODEL), F32),
        scratch_shapes=[pltpu.VMEM((tm, D_MODEL), F32), pltpu.VMEM((tm, hidden), BF16)],
        name="moe_final",
        compiler_params=_cparams(("parallel", "arbitrary")),
    )(h2, cmb, x1, g2, gain, wg, wu, wd)


def _rope_tables(pos):
    half = DK_RET // 2
    inv = ROPE_BASE ** (-jnp.arange(half, dtype=F32) / half)
    ang = pos.astype(F32)[:, None] * inv[None, :]
    cos, sin = jnp.cos(ang), jnp.sin(ang)
    zero = jnp.zeros_like(sin)
    tile = lambda a, b: jnp.tile(jnp.concatenate([a, b], axis=-1), (1, H_RET))
    return tile(cos, cos), tile(-sin, zero), tile(zero, sin)


def _pick_tile(n, target):
    t = min(n, target)
    while n % t:
        t //= 2
    return t


def kernel(x_prompt, x_sample, cache_k, cache_v, cache_logf, state_ret, page_table, c_prompt, c_sample, w_ada, b_ada, norm_mix, norm_ffn, w_in, b_fgt, w_br_fox, w_br_ret, w_out, w_route_group, b_route_group, w_route_expert, b_route_expert, w_exp_gate, w_exp_up, w_exp_down, norm_final):
    depth = w_ada.shape[0]
    assert depth == 1
    bp, seq, d = x_prompt.shape
    bd, lq, _ = x_sample.shape
    n_pool, page_size = cache_k.shape[1], cache_k.shape[2]
    n_pages = page_table.shape[1]
    past_len = n_pages * page_size
    tp, ts = bp * seq, bd * lq

    wl = w_in[0]
    w_perm = jnp.concatenate(
        [wl[:, 0:1536], wl[:, 1544:5128], wl[:, 1536:1544], jnp.zeros((d, LANES - H_FOX), F32)], axis=1).astype(BF16)
    b_fgt_pad = jnp.pad(b_fgt[0], (0, LANES - H_FOX)).reshape(1, LANES)
    gain_mix, gain_ffn, gain_fin = norm_mix[0].reshape(1, d), norm_ffn[0].reshape(1, d), norm_final.reshape(1, d)
    wf, wr, wo = w_br_fox[0].astype(BF16), w_br_ret[0].astype(BF16), w_out[0].astype(BF16)
    w_rt = jnp.pad(jnp.concatenate([w_route_group[0], w_route_expert[0]], axis=1),
                   ((0, 0), (0, LANES - N_GROUPS - N_EXPERTS)))
    wrh = w_rt.astype(BF16)
    wrl = (w_rt - wrh.astype(F32)).astype(BF16)
    b_rt = jnp.pad(jnp.concatenate([b_route_group[0], b_route_expert[0]]), (0, LANES - N_GROUPS - N_EXPERTS))
    b_rt = b_rt.reshape(1, LANES)
    wg, wu = w_exp_gate[0].astype(BF16), w_exp_up[0].astype(BF16)
    wd = w_exp_down[0].astype(BF16).reshape(N_GROUPS, EXPERTS_PER_GROUP * D_EXPERT, d)

    n_c = bp + bd
    n_c_pad = -(-n_c // 8) * 8
    c_all = jnp.pad(jnp.concatenate([c_prompt, c_sample], axis=0), ((0, n_c_pad - n_c), (0, 0)))
    mod = _ada_mod(c_all, w_ada[0], b_ada[0])
    mods = [mod[:, i * d:(i + 1) * d] for i in range(6)]
    tm_p = _pick_tile(seq, 512)
    tm_s = _pick_tile(ts, 512)
    mod_p = [m[:bp].reshape(bp, 1, d) for m in mods]
    mod_s = [jnp.repeat(m[bp:n_c], lq, axis=0).reshape(ts // tm_s, tm_s, d) for m in mods]

    xp = x_prompt.reshape(tp, d)
    tpb = seq // tm_p
    cos_p, sa_p, sb_p = _rope_tables(jnp.arange(seq))
    (qt, k_wide, kt, vt, vt_chunks, lf, qb, kb, vb, rg, ga, gb) = _in_proj(
        xp, mod_p[1], mod_p[0], gain_mix, w_perm, b_fgt_pad, cos_p, sa_p, sb_p, tm_p, tpb, tpb, True)
    logf_p = lf[:, :H_FOX].reshape(bp, seq, H_FOX)

    lf_rows = logf_p.transpose(0, 2, 1).reshape(bp * H_FOX, seq // LANES, LANES)
    c3 = jnp.stack([c.reshape(bp, H_FOX, seq) for c in _cumsum_split(lf_rows)], axis=2)
    c_rows = jnp.concatenate([c3, jnp.ones((bp, H_FOX, 3, seq), BF16),
                              jnp.zeros((bp, H_FOX, _C_ROWS - 6, seq), BF16)], axis=2)
    k_c = jnp.pad(-c3.transpose(0, 1, 3, 2), ((0, 0), (0, 0), (0, 0), (DH_FOX + 3, LANES - DH_FOX - 6)))
    tq = _pick_tile(seq, 512)
    yf_t = _fox_prompt(qt, c_rows, k_wide + k_c, vt_chunks, tq, 4).reshape(bp, D_FOX, seq)

    n_chunks = _pick_tile(seq // RET_CHUNK, 4)
    yr_p, rstate_p = _ret_prompt(qb, kb, vb, rg, bp, n_chunks)

    x1_p, h2_p, cmb_p = _mix(yf_t, yr_p, ga, gb, xp, mod_p[2], mod_p[4], mod_p[3], gain_ffn,
                             wf, wr, wo, wrh, wrl, b_rt, tm_p, tpb, True)
    y_p = _moe(h2_p, cmb_p, x1_p, mod_p[5], gain_fin, wg, wu, wd, tm_p, tpb)

    xs = x_sample.reshape(ts, d)
    cos_s, sa_s, sb_s = [jnp.tile(tb, (tm_s // lq, 1)) for tb in _rope_tables(past_len + jnp.arange(lq))]
    (qa_s, ka_s, va_s, lf_s, qb_s, kb_s, vb_s, rg_s, ga_s, gb_s) = _in_proj(
        xs, mod_s[1], mod_s[0], gain_mix, w_perm, b_fgt_pad, cos_s, sa_s, sb_s, tm_s, None, 1, False)
    logf_s = lf_s[:, :H_FOX].reshape(bd, lq, H_FOX)

    q4 = qa_s.reshape(bd, lq, H_FOX, DH_FOX).transpose(0, 2, 1, 3)
    eye = jnp.eye(H_FOX, dtype=BF16)
    qt_bd = (q4[:, :, :, None, :] * eye[None, :, None, :, None]).reshape(bd, H_FOX * lq, D_FOX)
    pps = _pick_tile(n_pages, 16)
    kt_pages = cache_k[0].transpose(0, 2, 3, 1).reshape(n_pool, D_FOX, page_size)
    vt_pages = cache_v[0].transpose(0, 2, 3, 1).reshape(n_pool, D_FOX, page_size)
    f_pages = _page_suffix(cache_logf[0].transpose(0, 2, 1).reshape(n_pool * H_FOX, page_size))
    yf_s = _fox_sample(page_table, kt_pages, vt_pages, f_pages,
                       qt_bd, ka_s.reshape(bd, lq, D_FOX).astype(BF16), va_s.reshape(bd, lq, D_FOX).astype(BF16),
                       logf_s.transpose(0, 2, 1), pps).reshape(ts, D_FOX)

    n_seq = _pick_tile(bd, 16)
    yr_s, rstate_s = _ret_sample(qb_s, kb_s, vb_s, rg_s, state_ret[0], lq, n_seq)

    x1_s, h2_s, cmb_s = _mix(yf_s, yr_s, ga_s, gb_s, xs, mod_s[2], mod_s[4], mod_s[3], gain_ffn,
                             wf, wr, wo, wrh, wrl, b_rt, tm_s, None, False)
    y_s = _moe(h2_s, cmb_s, x1_s, mod_s[5], gain_fin, wg, wu, wd, tm_s, None)

    return (y_p.reshape(bp, seq, d), y_s.reshape(bd, lq, d),
            kt.transpose(0, 3, 1, 2)[None], vt.transpose(0, 3, 1, 2)[None],
            logf_p[None], rstate_p[None],
            ka_s.reshape(1, bd, lq, H_FOX, DH_FOX), va_s.reshape(1, bd, lq, H_FOX, DH_FOX),
            logf_s[None], rstate_s[None])
```

```python
import functools

import numpy as np
import jax
import jax.numpy as jnp
from jax import lax
from jax.experimental import pallas as pl
from jax.experimental.pallas import tpu as pltpu

F32 = jnp.float32
BF16 = jnp.bfloat16

D_MODEL = 1024
H_FOX, DH_FOX = 8, 64
D_FOX = H_FOX * DH_FOX
H_RET, DK_RET, DV_RET = 4, 64, 128
D_RET_QK = H_RET * DK_RET
D_RET_V = H_RET * DV_RET
RET_CHUNK = 128
ROPE_BASE = 10000.0
N_GROUPS, EXPERTS_PER_GROUP, D_EXPERT = 4, 8, 256
N_EXPERTS = N_GROUPS * EXPERTS_PER_GROUP
EPS = 1e-6
LOG2E = 1.4426950408889634
NEG_BIG = -1e30
LANES = 128
KEY_STRIP = 256
V_ROWS = DH_FOX + 16

_C_QA, _C_KA, _C_VA = 0, 512, 1024
_C_QB, _C_KB, _C_VB, _C_RG, _C_GA, _C_GB, _C_FA, _C_END = 1536, 1792, 2048, 2560, 3072, 4096, 5120, 5248
_R_EXP0 = N_GROUPS

VMEM_LIMIT = 56 * 1024 * 1024


def _cparams(sem, vmem=VMEM_LIMIT, flags=None):
    return pltpu.CompilerParams(dimension_semantics=sem, vmem_limit_bytes=vmem, flags=flags)


def _split3(x):
    hi = x.astype(BF16)
    r1 = x - hi.astype(F32)
    mid = r1.astype(BF16)
    lo = (r1 - mid.astype(F32)).astype(BF16)
    return hi, mid, lo


def _dot(a, b):
    return jnp.dot(a, b, preferred_element_type=F32)


def _dot_nt(a, b):
    return lax.dot_general(a, b, (((1,), (1,)), ((), ())), preferred_element_type=F32)


def _dot_tn(a, b):
    return lax.dot_general(a, b, (((0,), (0,)), ((), ())), preferred_element_type=F32)


def _dot3_rhs_exact(x, m):
    return sum(_dot(p, m) for p in _split3(x))


def _dot3_lhs_exact(m, x):
    return sum(_dot(m, p) for p in _split3(x))


def _ada_kernel(c_ref, w_ref, b_ref, o_ref):
    a = jax.nn.silu(c_ref[...]).astype(BF16)
    o_ref[...] = _dot(a, w_ref[...].astype(BF16)) + b_ref[...]


def _ada_mod(c_all, w_ada, b_ada):
    rows = c_all.shape[0]
    n = w_ada.shape[1]
    tn = 1536
    return pl.pallas_call(
        _ada_kernel,
        grid=(n // tn,),
        in_specs=[pl.BlockSpec((rows, D_MODEL), lambda j: (0, 0)),
                  pl.BlockSpec((D_MODEL, tn), lambda j: (0, j)),
                  pl.BlockSpec((1, tn), lambda j: (0, j))],
        out_specs=pl.BlockSpec((rows, tn), lambda j: (0, j)),
        out_shape=jax.ShapeDtypeStruct((rows, n), F32),
        name="ada_mod",
        compiler_params=_cparams(("parallel",)),
    )(c_all, w_ada, b_ada.reshape(1, n))


def _inproj_kernel(x_ref, sc_ref, sh_ref, g_ref, w_ref, bf_ref, cos_ref, sa_ref, sb_ref, *out_refs, head_major):
    x = x_ref[...]
    tm = x.shape[0]
    y = x * lax.rsqrt(jnp.mean(x * x, axis=-1, keepdims=True) + EPS) * g_ref[...]
    h = (y * (1.0 + sc_ref[0]) + sh_ref[0]).astype(BF16)

    def proj(lo, hi):
        return _dot(h, w_ref[:, lo:hi])

    cos, sa, sb = cos_ref[...], sa_ref[...], sb_ref[...]

    def rope(t):
        return t * cos + pltpu.roll(t, D_RET_QK - DK_RET // 2, 1) * sa + pltpu.roll(t, DK_RET // 2, 1) * sb

    q = proj(_C_QA, _C_KA) * (DH_FOX ** -0.5 * LOG2E)
    k = proj(_C_KA, _C_VA)
    v = proj(_C_VA, _C_QB)
    if head_major:
        qt_ref, kw_ref, kt_ref, vt_ref, vc_ref = out_refs[:5]
        rest, t_scr = out_refs[5:-1], out_refs[-1]

        def transposed(a):
            t_scr[...] = a
            return t_scr[...].T

        qt_ref[0] = transposed(q).astype(BF16).reshape(H_FOX, DH_FOX, tm)
        kt_ref[0] = transposed(k).reshape(H_FOX, DH_FOX, tm)
        vt = transposed(v)
        vt_ref[0] = vt.reshape(H_FOX, DH_FOX, tm)
        vtb = vt.astype(BF16)
        lane = lax.broadcasted_iota(jnp.int32, (tm, LANES), 1)
        ones = jnp.where((lane >= DH_FOX) & (lane < DH_FOX + 3), 1.0, 0.0)
        srow = lax.broadcasted_iota(jnp.int32, (V_ROWS - DH_FOX, KEY_STRIP), 0)
        one_rows = jnp.where(srow == 0, 1.0, 0.0).astype(BF16)
        for hh in range(H_FOX):
            pair = k[:, (hh // 2) * LANES:(hh // 2 + 1) * LANES]
            if hh % 2:
                pair = pltpu.roll(pair, DH_FOX, 1)
            kw_ref[0, hh] = jnp.where(lane < DH_FOX, pair, ones).astype(BF16)
            for c in range(tm // KEY_STRIP):
                vc_ref[0, hh, c, :DH_FOX] = vtb[hh * DH_FOX:(hh + 1) * DH_FOX, c * KEY_STRIP:(c + 1) * KEY_STRIP]
                vc_ref[0, hh, c, DH_FOX:] = one_rows
    else:
        q_ref, k_ref, v_ref = out_refs[:3]
        rest = out_refs[3:]
        q_ref[...] = q.astype(BF16)
        k_ref[...] = k
        v_ref[...] = v
    lf_ref, qb_ref, kb_ref, vb_ref, rg_ref, ga_ref, gb_ref = rest
    qb_ref[...] = rope(proj(_C_QB, _C_KB)).astype(BF16)
    kb_ref[...] = (rope(proj(_C_KB, _C_VB)) * DK_RET ** -0.5).astype(BF16)
    vb_ref[...] = proj(_C_VB, _C_RG).astype(BF16)
    rg_ref[...] = proj(_C_RG, _C_GA)
    ga_ref[...] = proj(_C_GA, _C_GB)
    gb_ref[...] = proj(_C_GB, _C_FA)
    lf_ref[...] = jax.nn.log_sigmoid(proj(_C_FA, _C_END) + bf_ref[...])


def _mod_spec(mod, tiles_per_batch):
    r = mod.shape[1]
    if tiles_per_batch is None:
        return pl.BlockSpec((1, r, D_MODEL), lambda i: (i, 0, 0))
    return pl.BlockSpec((1, r, D_MODEL), lambda i: (i // tiles_per_batch, 0, 0))


def _in_proj(x, sc, sh, gain, w_perm, b_fgt_pad, cos, sa, sb, tm, tiles_per_batch, rope_tiles, head_major):
    t = x.shape[0]
    row = lambda n: pl.BlockSpec((tm, n), lambda i: (i, 0))
    const = lambda shp: pl.BlockSpec(shp, lambda i: (0,) * len(shp), pipeline_mode=pl.Buffered(1))
    rope_spec = pl.BlockSpec((tm, D_RET_QK), lambda i: (i % rope_tiles, 0))
    outs = [(LANES, F32), (D_RET_QK, BF16), (D_RET_QK, BF16), (D_RET_V, BF16), (D_RET_V, F32),
            (D_MODEL, F32), (D_MODEL, F32)]
    if head_major:
        tpb = tiles_per_batch
        nb, seq = t // (tpb * tm), tpb * tm
        tspec = pl.BlockSpec((1, H_FOX, DH_FOX, tm), lambda i: (i // tpb, 0, 0, i % tpb))
        head_specs = [tspec, pl.BlockSpec((1, H_FOX, tm, LANES), lambda i: (i // tpb, 0, i % tpb, 0)), tspec, tspec,
                      pl.BlockSpec((1, H_FOX, tm // KEY_STRIP, V_ROWS, KEY_STRIP),
                                   lambda i: (i // tpb, 0, i % tpb, 0, 0))]
        head_shapes = [jax.ShapeDtypeStruct((nb, H_FOX, DH_FOX, seq), BF16),
                       jax.ShapeDtypeStruct((nb, H_FOX, seq, LANES), BF16),
                       jax.ShapeDtypeStruct((nb, H_FOX, DH_FOX, seq), F32),
                       jax.ShapeDtypeStruct((nb, H_FOX, DH_FOX, seq), F32),
                       jax.ShapeDtypeStruct((nb, H_FOX, seq // KEY_STRIP, V_ROWS, KEY_STRIP), BF16)]
    else:
        head_specs = [row(D_FOX)] * 3
        head_shapes = [jax.ShapeDtypeStruct((t, D_FOX), dt) for dt in (BF16, F32, F32)]
    return pl.pallas_call(
        functools.partial(_inproj_kernel, head_major=head_major),
        grid=(t // tm,),
        in_specs=[row(D_MODEL), _mod_spec(sc, tiles_per_batch), _mod_spec(sh, tiles_per_batch),
                  const((1, D_MODEL)), const((D_MODEL, _C_END)), const((1, LANES)),
                  rope_spec, rope_spec, rope_spec],
        out_specs=head_specs + [row(n) for n, _ in outs],
        out_shape=head_shapes + [jax.ShapeDtypeStruct((t, n), dt) for n, dt in outs],
        scratch_shapes=[pltpu.VMEM((tm, D_FOX), F32)] if head_major else [],
        name="in_proj",
        compiler_params=_cparams(("parallel",)),
    )(x, sc, sh, gain, w_perm, b_fgt_pad, cos, sa, sb)


def _cumsum_kernel(x_ref, hi_ref, mid_ref, lo_ref):
    x = x_ref[0]
    r = x.shape[0]
    a = lax.broadcasted_iota(jnp.int32, (LANES, LANES), 0)
    b = lax.broadcasted_iota(jnp.int32, (LANES, LANES), 1)
    incl = (a <= b).astype(BF16)
    y = _dot3_rhs_exact(x, incl)
    tot = jnp.broadcast_to(y[:, LANES - 1:LANES], (r, LANES))
    ra = lax.broadcasted_iota(jnp.int32, (r, r), 0)
    rb = lax.broadcasted_iota(jnp.int32, (r, r), 1)
    strict = (rb < ra).astype(BF16)
    c = (y + _dot3_lhs_exact(strict, tot)) * LOG2E
    hi, mid, lo = _split3(c)
    hi_ref[0] = hi
    mid_ref[0] = mid
    lo_ref[0] = lo


def _cumsum_split(lf_rows):
    n, r, _ = lf_rows.shape
    spec = pl.BlockSpec((1, r, LANES), lambda i: (i, 0, 0))
    return pl.pallas_call(
        _cumsum_kernel,
        grid=(n,),
        in_specs=[spec],
        out_specs=[spec] * 3,
        out_shape=[jax.ShapeDtypeStruct((n, r, LANES), BF16)] * 3,
        name="logf_cumsum",
        compiler_params=_cparams(("parallel",)),
    )(lf_rows)


_C_ROWS = 16


def _fox_prompt_kernel(q_ref, ca_ref, k_ref, v_ref, o_ref, m_scr, acc_scr, *, tq, ks):
    i = pl.program_id(2)
    n_heads = q_ref.shape[1]
    per = tq // ks
    qts = [jnp.concatenate([q_ref[0, g], ca_ref[0, g], jnp.zeros((LANES - DH_FOX - _C_ROWS, tq), BF16)], axis=0)
           for g in range(n_heads)]

    m_scr[...] = jnp.full(m_scr.shape, NEG_BIG, F32)
    acc_scr[...] = jnp.zeros(acc_scr.shape, F32)

    def strips(c0, masked):
        cs = [c0 + u for u in range(per)]
        scores = [[_dot(k_ref[0, g, pl.ds(pl.multiple_of(c * ks, ks), ks), :], qts[g]) for g in range(n_heads)]
                  for c in cs]
        for u, c in enumerate(cs):
            for g in range(n_heads):
                s = scores[u][g]
                if masked:
                    kpos = c * ks + lax.broadcasted_iota(jnp.int32, (ks, tq), 0)
                    qpos = i * tq + lax.broadcasted_iota(jnp.int32, (ks, tq), 1)
                    s = jnp.where(kpos <= qpos, s, NEG_BIG)
                m = m_scr[g]
                m_new = jnp.maximum(m, jnp.max(s, axis=0, keepdims=True))
                alpha = jnp.exp2(m - m_new)
                p = jnp.exp2(s - m_new)
                acc_scr[g] = alpha * acc_scr[g] + _dot(v_ref[0, g, c], p.astype(BF16))
                m_scr[g] = m_new

    def block(j, carry):
        strips(j * per, False)
        return carry

    lax.fori_loop(0, i, block, 0)
    strips(i * per, True)
    for g in range(n_heads):
        o_ref[0, g] = (acc_scr[g, :DH_FOX] / acc_scr[g, DH_FOX:DH_FOX + 1]).astype(BF16)


def _fox_prompt(qt, c_rows, k_aug, vt_chunks, tq, heads_per_step):
    b, h, _, s = qt.shape
    g = heads_per_step
    ks = vt_chunks.shape[-1]
    qmap = lambda bi, hi, i: (bi, hi, 0, i)
    resident = lambda shp: pl.BlockSpec(shp, lambda bi, hi, i: (bi, hi) + (0,) * (len(shp) - 2),
                                        pipeline_mode=pl.Buffered(1))
    return pl.pallas_call(
        functools.partial(_fox_prompt_kernel, tq=tq, ks=ks),
        grid=(b, h // g, s // tq),
        in_specs=[pl.BlockSpec((1, g, DH_FOX, tq), qmap),
                  pl.BlockSpec((1, g, _C_ROWS, tq), qmap),
                  resident((1, g, s, LANES)),
                  resident((1, g, s // ks, V_ROWS, ks))],
        out_specs=pl.BlockSpec((1, g, DH_FOX, tq), qmap),
        out_shape=jax.ShapeDtypeStruct((b, h, DH_FOX, s), BF16),
        scratch_shapes=[pltpu.VMEM((g, 1, tq), F32), pltpu.VMEM((g, V_ROWS, tq), F32)],
        name="fox_prompt",
        compiler_params=_cparams(("parallel", "parallel", "arbitrary")),
    )(qt, c_rows, k_aug, vt_chunks)


def _fox_sample_kernel(pt_ref, *refs, pps, n_steps, lq, ps):
    k_refs = refs[:pps]
    v_refs = refs[pps:2 * pps]
    f_refs = refs[2 * pps:3 * pps]
    qt_ref, kn_ref, vn_ref, fn_ref, o_ref, m_scr, l_scr, acc_scr, car_scr = refs[3 * pps:]
    del pt_ref
    hq = H_FOX * lq
    step = pl.program_id(1)

    row_h = lax.broadcasted_iota(jnp.int32, (hq, H_FOX), 0) // lq
    col_h = lax.broadcasted_iota(jnp.int32, (hq, H_FOX), 1)
    rep = (row_h == col_h).astype(BF16)

    @pl.when(step == 0)
    def _():
        m_scr[...] = jnp.full(m_scr.shape, NEG_BIG, F32)
        l_scr[...] = jnp.zeros(l_scr.shape, F32)
        acc_scr[...] = jnp.zeros(acc_scr.shape, F32)
        car_scr[...] = jnp.zeros(car_scr.shape, F32)

    qt = qt_ref[0]

    def update(scores, pv_fn):
        mx = scores[0]
        for s in scores[1:]:
            mx = jnp.maximum(mx, s)
        m = m_scr[...]
        m_new = jnp.maximum(m, jnp.max(mx, axis=-1, keepdims=True))
        alpha = jnp.exp2(m - m_new)
        ps = [jnp.exp2(s - m_new) for s in scores]
        tot = ps[0]
        for p in ps[1:]:
            tot = tot + p
        l_scr[...] = alpha * l_scr[...] + jnp.sum(tot, axis=-1, keepdims=True)
        acc_scr[...] = alpha * acc_scr[...] + pv_fn([p.astype(BF16) for p in ps])
        m_scr[...] = m_new

    car = car_scr[...]
    biases = []
    for j in range(pps):
        f = f_refs[j][...]
        biases.append(f[:, :ps] + car)
        car = car + f[:, ps:]
    car_scr[...] = car
    bias_rows = _dot3_lhs_exact(rep, jnp.concatenate(biases, axis=1))
    scores = [_dot(qt, k_refs[j][0].astype(BF16)) + bias_rows[:, j * ps:(j + 1) * ps] for j in range(pps)]

    def pv_pages(ps_bf):
        pv = _dot_nt(ps_bf[0], v_refs[0][0].astype(BF16))
        for j in range(1, pps):
            pv = pv + _dot_nt(ps_bf[j], v_refs[j][0].astype(BF16))
        return pv

    update(scores, pv_pages)

    @pl.when(step == n_steps - 1)
    def _():
        na = lax.broadcasted_iota(jnp.int32, (lq, lq), 0)
        nb = lax.broadcasted_iota(jnp.int32, (lq, lq), 1)
        c_new = _dot3_rhs_exact(fn_ref[0], (na <= nb).astype(BF16)) * LOG2E
        cj = _dot3_lhs_exact(rep, c_new)
        qi = lax.broadcasted_iota(jnp.int32, (hq, lq), 0) % lq
        kj = lax.broadcasted_iota(jnp.int32, (hq, lq), 1)
        s = jnp.where(kj <= qi, _dot_nt(qt, kn_ref[0]) - cj, NEG_BIG)
        update([s], lambda ps_bf: _dot(ps_bf[0], vn_ref[0]))
        acc = acc_scr[...] / l_scr[...]
        rh = lax.broadcasted_iota(jnp.int32, (hq, D_FOX), 0) // lq
        ch = lax.broadcasted_iota(jnp.int32, (hq, D_FOX), 1) // DH_FOX
        acc = jnp.where(rh == ch, acc, 0.0)
        out = acc[0:lq]
        for h in range(1, H_FOX):
            out = out + acc[h * lq:(h + 1) * lq]
        o_ref[0] = out.astype(BF16)


def _fox_sample(page_table, kt_pages, vt_pages, f_pages, qt_bd, k_new, v_new, f_new_t, pps):
    bd, n_pages = page_table.shape
    lq = k_new.shape[1]
    hq = H_FOX * lq
    n_steps = n_pages // pps
    ps = kt_pages.shape[2]

    def page_map(j):
        return lambda b, p, pt: (pt[b, n_pages - 1 - (p * pps + j)], 0, 0)

    def page_map2(j):
        return lambda b, p, pt: (pt[b, n_pages - 1 - (p * pps + j)], 0)

    per_b = lambda shp: pl.BlockSpec((1,) + shp, lambda b, p, pt: (b, 0, 0))
    in_specs = ([pl.BlockSpec((1, D_FOX, ps), page_map(j)) for j in range(pps)]
                + [pl.BlockSpec((1, D_FOX, ps), page_map(j)) for j in range(pps)]
                + [pl.BlockSpec((H_FOX, 2 * ps), page_map2(j)) for j in range(pps)]
                + [per_b((hq, D_FOX)), per_b((lq, D_FOX)), per_b((lq, D_FOX)), per_b((H_FOX, lq))])
    grid_spec = pltpu.PrefetchScalarGridSpec(
        num_scalar_prefetch=1,
        grid=(bd, n_steps),
        in_specs=in_specs,
        out_specs=per_b((lq, D_FOX)),
        scratch_shapes=[pltpu.VMEM((hq, 1), F32), pltpu.VMEM((hq, 1), F32),
                        pltpu.VMEM((hq, D_FOX), F32), pltpu.VMEM((H_FOX, ps), F32)],
    )
    return pl.pallas_call(
        functools.partial(_fox_sample_kernel, pps=pps, n_steps=n_steps, lq=lq, ps=ps),
        grid_spec=grid_spec,
        out_shape=jax.ShapeDtypeStruct((bd, lq, D_FOX), BF16),
        name="fox_sample",
        compiler_params=_cparams(("parallel", "arbitrary")),
    )(page_table, *([kt_pages] * pps), *([vt_pages] * pps), *([f_pages] * pps), qt_bd, k_new, v_new, f_new_t)


def _page_suffix_kernel(x_ref, o_ref):
    x = x_ref[...]
    ps = x.shape[1]
    ja = lax.broadcasted_iota(jnp.int32, (ps, ps), 0)
    jb = lax.broadcasted_iota(jnp.int32, (ps, ps), 1)
    o_ref[:, :ps] = _dot3_rhs_exact(x, (ja > jb).astype(BF16)) * LOG2E
    o_ref[:, ps:] = _dot3_rhs_exact(x, jnp.ones((ps, ps), BF16)) * LOG2E


def _page_suffix(lf_rows):
    n, ps = lf_rows.shape
    tr = _pick_tile(n, 4096)
    return pl.pallas_call(
        _page_suffix_kernel,
        grid=(n // tr,),
        in_specs=[pl.BlockSpec((tr, ps), lambda i: (i, 0))],
        out_specs=pl.BlockSpec((tr, 2 * ps), lambda i: (i, 0)),
        out_shape=jax.ShapeDtypeStruct((n, 2 * ps), F32),
        name="page_suffix",
        compiler_params=_cparams(("parallel",)),
    )(lf_rows)


def _ret_head(q, k, v, rg, r, dec, qd, kd, cd):
    mm = q.dtype
    inner = _dot_nt(q, k) * dec
    o = _dot(inner.astype(mm), v) + qd * _dot(q, r.astype(mm))
    r_new = cd * r + _dot_tn((k.astype(F32) * kd).astype(mm), v)
    mu = jnp.mean(o, axis=-1, keepdims=True)
    var = jnp.mean(jnp.square(o - mu), axis=-1, keepdims=True)
    y = (o - mu) * lax.rsqrt(var + EPS)
    return (y * jax.nn.silu(rg)).astype(BF16), r_new


def _ret_prompt_kernel(q_ref, k_ref, v_ref, rg_ref, dec_ref, qd_ref, kd_ref, cd_ref, y_ref, rout_ref, r_scr,
                       *, chunk, n_chunks):
    step = pl.program_id(1)

    @pl.when(step == 0)
    def _():
        r_scr[...] = jnp.zeros(r_scr.shape, F32)

    for c in range(n_chunks):
        rows = slice(c * chunk, (c + 1) * chunk)
        for h in range(H_RET):
            qk = slice(h * DK_RET, (h + 1) * DK_RET)
            vv = slice(h * DV_RET, (h + 1) * DV_RET)
            y, r_new = _ret_head(q_ref[rows, qk], k_ref[rows, qk], v_ref[rows, vv], rg_ref[rows, vv], r_scr[h],
                                 dec_ref[h], qd_ref[h], kd_ref[h], cd_ref[h])
            y_ref[rows, vv] = y
            r_scr[h] = r_new

    @pl.when(step == pl.num_programs(1) - 1)
    def _():
        rout_ref[0] = r_scr[...]


def _ret_tables(chunk):
    lg = jnp.log1p(-jnp.exp2(-5.0 - jnp.arange(H_RET, dtype=F32)))
    pos = jnp.arange(chunk, dtype=F32)
    diff = pos[:, None] - pos[None, :]
    dec = jnp.where(diff >= 0, jnp.exp(lg[:, None, None] * jnp.maximum(diff, 0.0)), 0.0)
    q_dec = jnp.exp(lg[:, None] * (pos + 1.0))
    k_dec = jnp.exp(lg[:, None] * (chunk - 1.0 - pos))
    chunk_dec = jnp.exp(lg * chunk)
    qd = jnp.broadcast_to(q_dec[:, :, None], (H_RET, chunk, DV_RET))
    kd = jnp.broadcast_to(k_dec[:, :, None], (H_RET, chunk, DK_RET))
    cd = jnp.broadcast_to(chunk_dec[:, None, None], (H_RET, DK_RET, DV_RET))
    return dec, qd, kd, cd


def _ret_prompt(qb, kb, vb, rg, batch, n_chunks):
    t = qb.shape[0]
    chunk = RET_CHUNK
    rows = chunk * n_chunks
    steps = t // batch // rows
    row = lambda n: pl.BlockSpec((rows, n), lambda b, i: (b * steps + i, 0))
    const = lambda shp: pl.BlockSpec(shp, lambda b, i: (0,) * len(shp))
    tables = _ret_tables(chunk)
    return pl.pallas_call(
        functools.partial(_ret_prompt_kernel, chunk=chunk, n_chunks=n_chunks),
        grid=(batch, steps),
        in_specs=[row(D_RET_QK), row(D_RET_QK), row(D_RET_V), row(D_RET_V)] + [const(tb.shape) for tb in tables],
        out_specs=[row(D_RET_V), pl.BlockSpec((1, H_RET, DK_RET, DV_RET), lambda b, i: (b, 0, 0, 0))],
        out_shape=[jax.ShapeDtypeStruct((t, D_RET_V), BF16),
                   jax.ShapeDtypeStruct((batch, H_RET, DK_RET, DV_RET), F32)],
        scratch_shapes=[pltpu.VMEM((H_RET, DK_RET, DV_RET), F32)],
        name="ret_prompt",
        compiler_params=_cparams(("parallel", "arbitrary")),
    )(qb, kb, vb, rg, *tables)


def _ret_sample_kernel(q_ref, k_ref, v_ref, rg_ref, r0_ref, dec_ref, qd_ref, kd_ref, cd_ref, y_ref, rout_ref,
                       *, chunk, n_seq):
    for s in range(n_seq):
        rows = slice(s * chunk, (s + 1) * chunk)
        for h in range(H_RET):
            qk = slice(h * DK_RET, (h + 1) * DK_RET)
            vv = slice(h * DV_RET, (h + 1) * DV_RET)
            y, r_new = _ret_head(q_ref[rows, qk].astype(F32), k_ref[rows, qk].astype(F32),
                                 v_ref[rows, vv].astype(F32), rg_ref[rows, vv], r0_ref[s, h],
                                 dec_ref[h], qd_ref[h], kd_ref[h], cd_ref[h])
            y_ref[rows, vv] = y
            rout_ref[s, h] = r_new


def _ret_sample(qb, kb, vb, rg, r0, chunk, n_seq):
    t = qb.shape[0]
    rows = chunk * n_seq
    row = lambda n: pl.BlockSpec((rows, n), lambda i: (i, 0))
    const = lambda shp: pl.BlockSpec(shp, lambda i: (0,) * len(shp))
    st = pl.BlockSpec((n_seq, H_RET, DK_RET, DV_RET), lambda i: (i, 0, 0, 0))
    tables = _ret_tables(chunk)
    return pl.pallas_call(
        functools.partial(_ret_sample_kernel, chunk=chunk, n_seq=n_seq),
        grid=(t // rows,),
        in_specs=[row(D_RET_QK), row(D_RET_QK), row(D_RET_V), row(D_RET_V), st] + [const(tb.shape) for tb in tables],
        out_specs=[row(D_RET_V), st],
        out_shape=[jax.ShapeDtypeStruct((t, D_RET_V), BF16), jax.ShapeDtypeStruct(r0.shape, F32)],
        name="ret_sample",
        compiler_params=_cparams(("parallel",)),
    )(qb, kb, vb, rg, r0, *tables)


def _route(lg):
    lane = lax.broadcasted_iota(jnp.int32, lg.shape, 1)
    big = jnp.int32(1 << 20)
    is_grp = lane < N_GROUPS
    mg = jnp.max(jnp.where(is_grp, lg, -jnp.inf), axis=-1, keepdims=True)
    g_idx = jnp.min(jnp.where(is_grp & (lg == mg), lane, big), axis=-1, keepdims=True)
    g_w = 1.0 / jnp.sum(jnp.where(is_grp, jnp.exp(lg - mg), 0.0), axis=-1, keepdims=True)
    lo = _R_EXP0 + g_idx * EXPERTS_PER_GROUP
    sel = (lane >= lo) & (lane < lo + EXPERTS_PER_GROUP)
    v1 = jnp.max(jnp.where(sel, lg, -jnp.inf), axis=-1, keepdims=True)
    i1 = jnp.min(jnp.where(sel & (lg == v1), lane, big), axis=-1, keepdims=True)
    sel2 = sel & (lane != i1)
    v2 = jnp.max(jnp.where(sel2, lg, -jnp.inf), axis=-1, keepdims=True)
    i2 = jnp.min(jnp.where(sel2 & (lg == v2), lane, big), axis=-1, keepdims=True)
    e2 = jnp.exp(v2 - v1)
    w1 = g_w / (1.0 + e2)
    w2 = g_w * e2 / (1.0 + e2)
    return jnp.where(lane == i1, w1, 0.0) + jnp.where(lane == i2, w2, 0.0)


def _mix_kernel(yf_ref, yr_ref, ga_ref, gb_ref, x_ref, g1_ref, sc_ref, sh_ref, gain_ref,
                wf_ref, wr_ref, wo_ref, wrh_ref, wrl_ref, br_ref,
                x1_ref, h2_ref, cmb_ref, *, fox_transposed):
    if fox_transposed:
        pf = _dot_tn(yf_ref[0], wf_ref[...])
    else:
        pf = _dot(yf_ref[...], wf_ref[...])
    pr = _dot(yr_ref[...], wr_ref[...])
    mixed = jax.nn.sigmoid(ga_ref[...]) * pf + jax.nn.sigmoid(gb_ref[...]) * pr
    x1 = x_ref[...] + g1_ref[0] * _dot(mixed.astype(BF16), wo_ref[...])
    x1_ref[...] = x1
    y = x1 * lax.rsqrt(jnp.mean(x1 * x1, axis=-1, keepdims=True) + EPS) * gain_ref[...]
    h2 = y * (1.0 + sc_ref[0]) + sh_ref[0]
    h2_ref[...] = h2.astype(BF16)
    h_hi = h2.astype(BF16)
    h_lo = (h2 - h_hi.astype(F32)).astype(BF16)
    lg = _dot(h_hi, wrh_ref[...]) + (_dot(h_hi, wrl_ref[...]) + _dot(h_lo, wrh_ref[...])) + br_ref[...]
    cmb_ref[...] = _route(lg)


def _mix(yf, yr, ga, gb, x, g1, sc2, sh2, gain, wf, wr, wo, wrh, wrl, br, tm, tiles_per_batch, fox_transposed):
    t = x.shape[0]
    row = lambda n: pl.BlockSpec((tm, n), lambda i: (i, 0))
    const = lambda shp: pl.BlockSpec(shp, lambda i: (0,) * len(shp))
    if fox_transposed:
        yf_spec = pl.BlockSpec((1, D_FOX, tm), lambda i: (i // tiles_per_batch, 0, i % tiles_per_batch))
    else:
        yf_spec = row(D_FOX)
    ms = lambda m: _mod_spec(m, tiles_per_batch)
    return pl.pallas_call(
        functools.partial(_mix_kernel, fox_transposed=fox_transposed),
        grid=(t // tm,),
        in_specs=[yf_spec, row(D_RET_V), row(D_MODEL), row(D_MODEL), row(D_MODEL), ms(g1), ms(sc2), ms(sh2),
                  const((1, D_MODEL)), const((D_FOX, D_MODEL)), const((D_RET_V, D_MODEL)),
                  const((D_MODEL, D_MODEL)), const((D_MODEL, LANES)), const((D_MODEL, LANES)), const((1, LANES))],
        out_specs=[row(D_MODEL), row(D_MODEL), row(LANES)],
        out_shape=[jax.ShapeDtypeStruct((t, D_MODEL), F32), jax.ShapeDtypeStruct((t, D_MODEL), BF16),
                   jax.ShapeDtypeStruct((t, LANES), F32)],
        name="mix_route",
        compiler_params=_cparams(("parallel",)),
    )(yf, yr, ga, gb, x, g1, sc2, sh2, gain, wf, wr, wo, wrh, wrl, br)


def _moe_kernel(h_ref, cmb_ref, x1_ref, g2_ref, gain_ref, wg_ref, wu_ref, wd_ref, y_ref, acc_scr, hid_scr):
    g = pl.program_id(1)

    @pl.when(g == 0)
    def _():
        acc_scr[...] = jnp.zeros(acc_scr.shape, F32)

    h = h_ref[...]
    cmb = cmb_ref[...]
    lane = lax.broadcasted_iota(jnp.int32, cmb.shape, 1)
    for e in range(EXPERTS_PER_GROUP):
        a = _dot(h, wg_ref[0, e])
        u = _dot(h, wu_ref[0, e])
        w = jnp.sum(jnp.where(lane == _R_EXP0 + g * EXPERTS_PER_GROUP + e, cmb, 0.0), axis=-1, keepdims=True)
        hid_scr[:, e * D_EXPERT:(e + 1) * D_EXPERT] = (jax.nn.silu(a) * u * w).astype(BF16)
    acc_scr[...] += _dot(hid_scr[...], wd_ref[0])

    @pl.when(g == N_GROUPS - 1)
    def _():
        x2 = x1_ref[...] + g2_ref[0] * acc_scr[...]
        y_ref[...] = x2 * lax.rsqrt(jnp.mean(x2 * x2, axis=-1, keepdims=True) + EPS) * gain_ref[...]


def _moe(h2, cmb, x1, g2, gain, wg, wu, wd, tm, tiles_per_batch):
    t = h2.shape[0]
    row = lambda n: pl.BlockSpec((tm, n), lambda i, g: (i, 0))
    r = g2.shape[1]
    if tiles_per_batch is None:
        g2_spec = pl.BlockSpec((1, r, D_MODEL), lambda i, g: (i, 0, 0))
    else:
        g2_spec = pl.BlockSpec((1, r, D_MODEL), lambda i, g: (i // tiles_per_batch, 0, 0))
    hidden = EXPERTS_PER_GROUP * D_EXPERT
    return pl.pallas_call(
        _moe_kernel,
        grid=(t // tm, N_GROUPS),
        in_specs=[row(D_MODEL), row(LANES), row(D_MODEL), g2_spec,
                  pl.BlockSpec((1, D_MODEL), lambda i, g: (0, 0)),
                  pl.BlockSpec((1, EXPERTS_PER_GROUP, D_MODEL, D_EXPERT), lambda i, g: (g, 0, 0, 0)),
                  pl.BlockSpec((1, EXPERTS_PER_GROUP, D_MODEL, D_EXPERT), lambda i, g: (g, 0, 0, 0)),
                  pl.BlockSpec((1, hidden, D_MODEL), lambda i, g: (g, 0, 0))],
        out_specs=row(D_MODEL),
        out_shape=jax.ShapeDtypeStruct((t, D_MODEL), F32),
        scratch_shapes=[pltpu.VMEM((tm, D_MODEL), F32), pltpu.VMEM((tm, hidden), BF16)],
        name="moe_final",
        compiler_params=_cparams(("parallel", "arbitrary")),
    )(h2, cmb, x1, g2, gain, wg, wu, wd)


def _rope_tables(pos):
    half = DK_RET // 2
    inv = ROPE_BASE ** (-jnp.arange(half, dtype=F32) / half)
    ang = pos.astype(F32)[:, None] * inv[None, :]
    cos, sin = jnp.cos(ang), jnp.sin(ang)
    zero = jnp.zeros_like(sin)
    tile = lambda a, b: jnp.tile(jnp.concatenate([a, b], axis=-1), (1, H_RET))
    return tile(cos, cos), tile(-sin, zero), tile(zero, sin)


def _pick_tile(n, target):
    t = min(n, target)
    while n % t:
        t //= 2
    return t


def kernel(x_prompt, x_sample, cache_k, cache_v, cache_logf, state_ret, page_table, c_prompt, c_sample, w_ada, b_ada, norm_mix, norm_ffn, w_in, b_fgt, w_br_fox, w_br_ret, w_out, w_route_group, b_route_group, w_route_expert, b_route_expert, w_exp_gate, w_exp_up, w_exp_down, norm_final):
    depth = w_ada.shape[0]
    assert depth == 1
    bp, seq, d = x_prompt.shape
    bd, lq, _ = x_sample.shape
    n_pool, page_size = cache_k.shape[1], cache_k.shape[2]
    n_pages = page_table.shape[1]
    past_len = n_pages * page_size
    tp, ts = bp * seq, bd * lq

    wl = w_in[0]
    w_perm = jnp.concatenate(
        [wl[:, 0:1536], wl[:, 1544:5128], wl[:, 1536:1544], jnp.zeros((d, LANES - H_FOX), F32)], axis=1).astype(BF16)
    b_fgt_pad = jnp.pad(b_fgt[0], (0, LANES - H_FOX)).reshape(1, LANES)
    gain_mix, gain_ffn, gain_fin = norm_mix[0].reshape(1, d), norm_ffn[0].reshape(1, d), norm_final.reshape(1, d)
    wf, wr, wo = w_br_fox[0].astype(BF16), w_br_ret[0].astype(BF16), w_out[0].astype(BF16)
    w_rt = jnp.pad(jnp.concatenate([w_route_group[0], w_route_expert[0]], axis=1),
                   ((0, 0), (0, LANES - N_GROUPS - N_EXPERTS)))
    wrh = w_rt.astype(BF16)
    wrl = (w_rt - wrh.astype(F32)).astype(BF16)
    b_rt = jnp.pad(jnp.concatenate([b_route_group[0], b_route_expert[0]]), (0, LANES - N_GROUPS - N_EXPERTS))
    b_rt = b_rt.reshape(1, LANES)
    wg, wu = w_exp_gate[0].astype(BF16), w_exp_up[0].astype(BF16)
    wd = w_exp_down[0].astype(BF16).reshape(N_GROUPS, EXPERTS_PER_GROUP * D_EXPERT, d)

    n_c = bp + bd
    n_c_pad = -(-n_c // 8) * 8
    c_all = jnp.pad(jnp.concatenate([c_prompt, c_sample], axis=0), ((0, n_c_pad - n_c), (0, 0)))
    mod = _ada_mod(c_all, w_ada[0], b_ada[0])
    mods = [mod[:, i * d:(i + 1) * d] for i in range(6)]
    tm_p = _pick_tile(seq, 512)
    tm_s = _pick_tile(ts, 512)
    mod_p = [m[:bp].reshape(bp, 1, d) for m in mods]
    mod_s = [jnp.repeat(m[bp:n_c], lq, axis=0).reshape(ts // tm_s, tm_s, d) for m in mods]

    xp = x_prompt.reshape(tp, d)
    tpb = seq // tm_p
    cos_p, sa_p, sb_p = _rope_tables(jnp.arange(seq))
    (qt, k_wide, kt, vt, vt_chunks, lf, qb, kb, vb, rg, ga, gb) = _in_proj(
        xp, mod_p[1], mod_p[0], gain_mix, w_perm, b_fgt_pad, cos_p, sa_p, sb_p, tm_p, tpb, tpb, True)
    logf_p = lf[:, :H_FOX].reshape(bp, seq, H_FOX)

    lf_rows = logf_p.transpose(0, 2, 1).reshape(bp * H_FOX, seq // LANES, LANES)
    c3 = jnp.stack([c.reshape(bp, H_FOX, seq) for c in _cumsum_split(lf_rows)], axis=2)
    c_rows = jnp.concatenate([c3, jnp.ones((bp, H_FOX, 3, seq), BF16),
                              jnp.zeros((bp, H_FOX, _C_ROWS - 6, seq), BF16)], axis=2)
    k_c = jnp.pad(-c3.transpose(0, 1, 3, 2), ((0, 0), (0, 0), (0, 0), (DH_FOX + 3, LANES - DH_FOX - 6)))
    tq = _pick_tile(seq, 512)
    yf_t = _fox_prompt(qt, c_rows, k_wide + k_c, vt_chunks, tq, 4).reshape(bp, D_FOX, seq)

    n_chunks = _pick_tile(seq // RET_CHUNK, 4)
    yr_p, rstate_p = _ret_prompt(qb, kb, vb, rg, bp, n_chunks)

    x1_p, h2_p, cmb_p = _mix(yf_t, yr_p, ga, gb, xp, mod_p[2], mod_p[4], mod_p[3], gain_ffn,
                             wf, wr, wo, wrh, wrl, b_rt, tm_p, tpb, True)
    y_p = _moe(h2_p, cmb_p, x1_p, mod_p[5], gain_fin, wg, wu, wd, tm_p, tpb)

    xs = x_sample.reshape(ts, d)
    cos_s, sa_s, sb_s = [jnp.tile(tb, (tm_s // lq, 1)) for tb in _rope_tables(past_len + jnp.arange(lq))]
    (qa_s, ka_s, va_s, lf_s, qb_s, kb_s, vb_s, rg_s, ga_s, gb_s) = _in_proj(
        xs, mod_s[1], mod_s[0], gain_mix, w_perm, b_fgt_pad, cos_s, sa_s, sb_s, tm_s, None, 1, False)
    logf_s = lf_s[:, :H_FOX].reshape(bd, lq, H_FOX)

    q4 = qa_s.reshape(bd, lq, H_FOX, DH_FOX).transpose(0, 2, 1, 3)
    eye = jnp.eye(H_FOX, dtype=BF16)
    qt_bd = (q4[:, :, :, None, :] * eye[None, :, None, :, None]).reshape(bd, H_FOX * lq, D_FOX)
    pps = _pick_tile(n_pages, 16)
    kt_pages = cache_k[0].transpose(0, 2, 3, 1).reshape(n_pool, D_FOX, page_size)
    vt_pages = cache_v[0].transpose(0, 2, 3, 1).reshape(n_pool, D_FOX, page_size)
    f_pages = _page_suffix(cache_logf[0].transpose(0, 2, 1).reshape(n_pool * H_FOX, page_size))
    yf_s = _fox_sample(page_table, kt_pages, vt_pages, f_pages,
                       qt_bd, ka_s.reshape(bd, lq, D_FOX).astype(BF16), va_s.reshape(bd, lq, D_FOX).astype(BF16),
                       logf_s.transpose(0, 2, 1), pps).reshape(ts, D_FOX)

    n_seq = _pick_tile(bd, 16)
    yr_s, rstate_s = _ret_sample(qb_s, kb_s, vb_s, rg_s, state_ret[0], lq, n_seq)

    x1_s, h2_s, cmb_s = _mix(yf_s, yr_s, ga_s, gb_s, xs, mod_s[2], mod_s[4], mod_s[3], gain_ffn,
                             wf, wr, wo, wrh, wrl, b_rt, tm_s, None, False)
    y_s = _moe(h2_s, cmb_s, x1_s, mod_s[5], gain_fin, wg, wu, wd, tm_s, None)

    return (y_p.reshape(bp, seq, d), y_s.reshape(bd, lq, d),
            kt.transpose(0, 3, 1, 2)[None], vt.transpose(0, 3, 1, 2)[None],
            logf_p[None], rstate_p[None],
            ka_s.reshape(1, bd, lq, H_FOX, DH_FOX), va_s.reshape(1, bd, lq, H_FOX, DH_FOX),
            logf_s[None], rstate_s[None])
```

```python
import functools

import numpy as np
import jax
import jax.numpy as jnp
from jax import lax
from jax.experimental import pallas as pl
from jax.experimental.pallas import tpu as pltpu

F32 = jnp.float32
BF16 = jnp.bfloat16

D_MODEL = 1024
H_FOX, DH_FOX = 8, 64
D_FOX = H_FOX * DH_FOX
H_RET, DK_RET, DV_RET = 4, 64, 128
D_RET_QK = H_RET * DK_RET
D_RET_V = H_RET * DV_RET
RET_CHUNK = 128
ROPE_BASE = 10000.0
N_GROUPS, EXPERTS_PER_GROUP, D_EXPERT = 4, 8, 256
N_EXPERTS = N_GROUPS * EXPERTS_PER_GROUP
EPS = 1e-6
LOG2E = 1.4426950408889634
NEG_BIG = -1e30
LANES = 128
KEY_STRIP = 256
V_ROWS = DH_FOX + 16

_C_QA, _C_KA, _C_VA = 0, 512, 1024
_C_QB, _C_KB, _C_VB, _C_RG, _C_GA, _C_GB, _C_FA, _C_END = 1536, 1792, 2048, 2560, 3072, 4096, 5120, 5248
_R_EXP0 = N_GROUPS
_R_GID = LANES - 1
MOE_TILE = 1024
MOE_SPLIT = 2
MOE_BIG_CHUNK = 4
MOE_ROW_BLOCK = 64

VMEM_LIMIT = 56 * 1024 * 1024


def _cparams(sem, vmem=VMEM_LIMIT, flags=None):
    return pltpu.CompilerParams(dimension_semantics=sem, vmem_limit_bytes=vmem, flags=flags)


def _split3(x):
    hi = x.astype(BF16)
    r1 = x - hi.astype(F32)
    mid = r1.astype(BF16)
    lo = (r1 - mid.astype(F32)).astype(BF16)
    return hi, mid, lo


def _dot(a, b):
    return jnp.dot(a, b, preferred_element_type=F32)


def _dot_nt(a, b):
    return lax.dot_general(a, b, (((1,), (1,)), ((), ())), preferred_element_type=F32)


def _dot_tn(a, b):
    return lax.dot_general(a, b, (((0,), (0,)), ((), ())), preferred_element_type=F32)


def _dot3_rhs_exact(x, m):
    return sum(_dot(p, m) for p in _split3(x))


def _dot3_lhs_exact(m, x):
    return sum(_dot(m, p) for p in _split3(x))


def _ada_kernel(c_ref, w_ref, b_ref, o_ref):
    a = jax.nn.silu(c_ref[...]).astype(BF16)
    o_ref[...] = _dot(a, w_ref[...].astype(BF16)) + b_ref[...]


def _ada_mod(c_all, w_ada, b_ada):
    rows = c_all.shape[0]
    n = w_ada.shape[1]
    tn = 1536
    return pl.pallas_call(
        _ada_kernel,
        grid=(n // tn,),
        in_specs=[pl.BlockSpec((rows, D_MODEL), lambda j: (0, 0)),
                  pl.BlockSpec((D_MODEL, tn), lambda j: (0, j)),
                  pl.BlockSpec((1, tn), lambda j: (0, j))],
        out_specs=pl.BlockSpec((rows, tn), lambda j: (0, j)),
        out_shape=jax.ShapeDtypeStruct((rows, n), F32),
        name="ada_mod",
        compiler_params=_cparams(("parallel",)),
    )(c_all, w_ada, b_ada.reshape(1, n))


def _inproj_kernel(x_ref, sc_ref, sh_ref, g_ref, w_ref, bf_ref, cos_ref, sa_ref, sb_ref, *out_refs, head_major):
    x = x_ref[...]
    tm = x.shape[0]
    y = x * lax.rsqrt(jnp.mean(x * x, axis=-1, keepdims=True) + EPS) * g_ref[...]
    h = (y * (1.0 + sc_ref[0]) + sh_ref[0]).astype(BF16)

    def proj(lo, hi):
        return _dot(h, w_ref[:, lo:hi])

    cos, sa, sb = cos_ref[...], sa_ref[...], sb_ref[...]

    def rope(t):
        return t * cos + pltpu.roll(t, D_RET_QK - DK_RET // 2, 1) * sa + pltpu.roll(t, DK_RET // 2, 1) * sb

    q = proj(_C_QA, _C_KA) * (DH_FOX ** -0.5 * LOG2E)
    k = proj(_C_KA, _C_VA)
    v = proj(_C_VA, _C_QB)
    if head_major:
        qt_ref, kw_ref, kt_ref, vt_ref, vc_ref = out_refs[:5]
        rest, t_scr = out_refs[5:-1], out_refs[-1]

        def transposed(a):
            t_scr[...] = a
            return t_scr[...].T

        qt_ref[0] = transposed(q).astype(BF16).reshape(H_FOX, DH_FOX, tm)
        kt_ref[0] = transposed(k).reshape(H_FOX, DH_FOX, tm)
        vt = transposed(v)
        vt_ref[0] = vt.reshape(H_FOX, DH_FOX, tm)
        vtb = vt.astype(BF16)
        lane = lax.broadcasted_iota(jnp.int32, (tm, LANES), 1)
        ones = jnp.where((lane >= DH_FOX) & (lane < DH_FOX + 3), 1.0, 0.0)
        srow = lax.broadcasted_iota(jnp.int32, (V_ROWS - DH_FOX, KEY_STRIP), 0)
        one_rows = jnp.where(srow == 0, 1.0, 0.0).astype(BF16)
        for hh in range(H_FOX):
            pair = k[:, (hh // 2) * LANES:(hh // 2 + 1) * LANES]
            if hh % 2:
                pair = pltpu.roll(pair, DH_FOX, 1)
            kw_ref[0, hh] = jnp.where(lane < DH_FOX, pair, ones).astype(BF16)
            for c in range(tm // KEY_STRIP):
                vc_ref[0, hh, c, :DH_FOX] = vtb[hh * DH_FOX:(hh + 1) * DH_FOX, c * KEY_STRIP:(c + 1) * KEY_STRIP]
                vc_ref[0, hh, c, DH_FOX:] = one_rows
    else:
        q_ref, k_ref, v_ref = out_refs[:3]
        rest = out_refs[3:]
        q_ref[...] = q.astype(BF16)
        k_ref[...] = k
        v_ref[...] = v
    lf_ref, qb_ref, kb_ref, vb_ref, rg_ref, ga_ref, gb_ref = rest
    qb_ref[...] = rope(proj(_C_QB, _C_KB)).astype(BF16)
    kb_ref[...] = (rope(proj(_C_KB, _C_VB)) * DK_RET ** -0.5).astype(BF16)
    vb_ref[...] = proj(_C_VB, _C_RG).astype(BF16)
    rg_ref[...] = proj(_C_RG, _C_GA)
    ga_ref[...] = proj(_C_GA, _C_GB)
    gb_ref[...] = proj(_C_GB, _C_FA)
    lf_ref[...] = jax.nn.log_sigmoid(proj(_C_FA, _C_END) + bf_ref[...])


def _mod_spec(mod, tiles_per_batch):
    r = mod.shape[1]
    if tiles_per_batch is None:
        return pl.BlockSpec((1, r, D_MODEL), lambda i: (i, 0, 0))
    return pl.BlockSpec((1, r, D_MODEL), lambda i: (i // tiles_per_batch, 0, 0))


def _in_proj(x, sc, sh, gain, w_perm, b_fgt_pad, cos, sa, sb, tm, tiles_per_batch, rope_tiles, head_major):
    t = x.shape[0]
    row = lambda n: pl.BlockSpec((tm, n), lambda i: (i, 0))
    const = lambda shp: pl.BlockSpec(shp, lambda i: (0,) * len(shp), pipeline_mode=pl.Buffered(1))
    rope_spec = pl.BlockSpec((tm, D_RET_QK), lambda i: (i % rope_tiles, 0))
    outs = [(LANES, F32), (D_RET_QK, BF16), (D_RET_QK, BF16), (D_RET_V, BF16), (D_RET_V, F32),
            (D_MODEL, F32), (D_MODEL, F32)]
    if head_major:
        tpb = tiles_per_batch
        nb, seq = t // (tpb * tm), tpb * tm
        tspec = pl.BlockSpec((1, H_FOX, DH_FOX, tm), lambda i: (i // tpb, 0, 0, i % tpb))
        head_specs = [tspec, pl.BlockSpec((1, H_FOX, tm, LANES), lambda i: (i // tpb, 0, i % tpb, 0)), tspec, tspec,
                      pl.BlockSpec((1, H_FOX, tm // KEY_STRIP, V_ROWS, KEY_STRIP),
                                   lambda i: (i // tpb, 0, i % tpb, 0, 0))]
        head_shapes = [jax.ShapeDtypeStruct((nb, H_FOX, DH_FOX, seq), BF16),
                       jax.ShapeDtypeStruct((nb, H_FOX, seq, LANES), BF16),
                       jax.ShapeDtypeStruct((nb, H_FOX, DH_FOX, seq), F32),
                       jax.ShapeDtypeStruct((nb, H_FOX, DH_FOX, seq), F32),
                       jax.ShapeDtypeStruct((nb, H_FOX, seq // KEY_STRIP, V_ROWS, KEY_STRIP), BF16)]
    else:
        head_specs = [row(D_FOX)] * 3
        head_shapes = [jax.ShapeDtypeStruct((t, D_FOX), dt) for dt in (BF16, F32, F32)]
    return pl.pallas_call(
        functools.partial(_inproj_kernel, head_major=head_major),
        grid=(t // tm,),
        in_specs=[row(D_MODEL), _mod_spec(sc, tiles_per_batch), _mod_spec(sh, tiles_per_batch),
                  const((1, D_MODEL)), const((D_MODEL, _C_END)), const((1, LANES)),
                  rope_spec, rope_spec, rope_spec],
        out_specs=head_specs + [row(n) for n, _ in outs],
        out_shape=head_shapes + [jax.ShapeDtypeStruct((t, n), dt) for n, dt in outs],
        scratch_shapes=[pltpu.VMEM((tm, D_FOX), F32)] if head_major else [],
        name="in_proj",
        compiler_params=_cparams(("parallel",)),
    )(x, sc, sh, gain, w_perm, b_fgt_pad, cos, sa, sb)


def _cumsum_kernel(x_ref, hi_ref, mid_ref, lo_ref):
    x = x_ref[0]
    r = x.shape[0]
    a = lax.broadcasted_iota(jnp.int32, (LANES, LANES), 0)
    b = lax.broadcasted_iota(jnp.int32, (LANES, LANES), 1)
    incl = (a <= b).astype(BF16)
    y = _dot3_rhs_exact(x, incl)
    tot = jnp.broadcast_to(y[:, LANES - 1:LANES], (r, LANES))
    ra = lax.broadcasted_iota(jnp.int32, (r, r), 0)
    rb = lax.broadcasted_iota(jnp.int32, (r, r), 1)
    strict = (rb < ra).astype(BF16)
    c = (y + _dot3_lhs_exact(strict, tot)) * LOG2E
    hi, mid, lo = _split3(c)
    hi_ref[0] = hi
    mid_ref[0] = mid
    lo_ref[0] = lo


def _cumsum_split(lf_rows):
    n, r, _ = lf_rows.shape
    spec = pl.BlockSpec((1, r, LANES), lambda i: (i, 0, 0))
    return pl.pallas_call(
        _cumsum_kernel,
        grid=(n,),
        in_specs=[spec],
        out_specs=[spec] * 3,
        out_shape=[jax.ShapeDtypeStruct((n, r, LANES), BF16)] * 3,
        name="logf_cumsum",
        compiler_params=_cparams(("parallel",)),
    )(lf_rows)


_C_ROWS = 16


def _fox_prompt_kernel(q_ref, ca_ref, k_ref, v_ref, o_ref, m_scr, acc_scr, *, tq, ks):
    i = pl.program_id(2)
    n_heads = q_ref.shape[1]
    per = tq // ks
    qts = [jnp.concatenate([q_ref[0, g], ca_ref[0, g], jnp.zeros((LANES - DH_FOX - _C_ROWS, tq), BF16)], axis=0)
           for g in range(n_heads)]

    m_scr[...] = jnp.full(m_scr.shape, NEG_BIG, F32)
    acc_scr[...] = jnp.zeros(acc_scr.shape, F32)

    def strips(c0, masked):
        cs = [c0 + u for u in range(per)]
        scores = [[_dot(k_ref[0, g, pl.ds(pl.multiple_of(c * ks, ks), ks), :], qts[g]) for g in range(n_heads)]
                  for c in cs]
        for u, c in enumerate(cs):
            for g in range(n_heads):
                s = scores[u][g]
                if masked:
                    kpos = c * ks + lax.broadcasted_iota(jnp.int32, (ks, tq), 0)
                    qpos = i * tq + lax.broadcasted_iota(jnp.int32, (ks, tq), 1)
                    s = jnp.where(kpos <= qpos, s, NEG_BIG)
                m = m_scr[g]
                m_new = jnp.maximum(m, jnp.max(s, axis=0, keepdims=True))
                alpha = jnp.exp2(m - m_new)
                p = jnp.exp2(s - m_new)
                acc_scr[g] = alpha * acc_scr[g] + _dot(v_ref[0, g, c], p.astype(BF16))
                m_scr[g] = m_new

    def block(j, carry):
        strips(j * per, False)
        return carry

    lax.fori_loop(0, i, block, 0)
    strips(i * per, True)
    for g in range(n_heads):
        o_ref[0, g] = (acc_scr[g, :DH_FOX] / acc_scr[g, DH_FOX:DH_FOX + 1]).astype(BF16)


def _fox_prompt(qt, c_rows, k_aug, vt_chunks, tq, heads_per_step):
    b, h, _, s = qt.shape
    g = heads_per_step
    ks = vt_chunks.shape[-1]
    qmap = lambda bi, hi, i: (bi, hi, 0, i)
    resident = lambda shp: pl.BlockSpec(shp, lambda bi, hi, i: (bi, hi) + (0,) * (len(shp) - 2),
                                        pipeline_mode=pl.Buffered(1))
    return pl.pallas_call(
        functools.partial(_fox_prompt_kernel, tq=tq, ks=ks),
        grid=(b, h // g, s // tq),
        in_specs=[pl.BlockSpec((1, g, DH_FOX, tq), qmap),
                  pl.BlockSpec((1, g, _C_ROWS, tq), qmap),
                  resident((1, g, s, LANES)),
                  resident((1, g, s // ks, V_ROWS, ks))],
        out_specs=pl.BlockSpec((1, g, DH_FOX, tq), qmap),
        out_shape=jax.ShapeDtypeStruct((b, h, DH_FOX, s), BF16),
        scratch_shapes=[pltpu.VMEM((g, 1, tq), F32), pltpu.VMEM((g, V_ROWS, tq), F32)],
        name="fox_prompt",
        compiler_params=_cparams(("parallel", "parallel", "arbitrary")),
    )(qt, c_rows, k_aug, vt_chunks)


def _fox_sample_kernel(pt_ref, *refs, pps, n_steps, lq, ps):
    k_refs = refs[:pps]
    v_refs = refs[pps:2 * pps]
    f_refs = refs[2 * pps:3 * pps]
    qt_ref, kn_ref, vn_ref, fn_ref, o_ref, m_scr, l_scr, acc_scr, car_scr = refs[3 * pps:]
    del pt_ref
    hq = H_FOX * lq
    step = pl.program_id(1)

    row_h = lax.broadcasted_iota(jnp.int32, (hq, H_FOX), 0) // lq
    col_h = lax.broadcasted_iota(jnp.int32, (hq, H_FOX), 1)
    rep = (row_h == col_h).astype(BF16)

    @pl.when(step == 0)
    def _():
        m_scr[...] = jnp.full(m_scr.shape, NEG_BIG, F32)
        l_scr[...] = jnp.zeros(l_scr.shape, F32)
        acc_scr[...] = jnp.zeros(acc_scr.shape, F32)
        car_scr[...] = jnp.zeros(car_scr.shape, F32)

    qt = qt_ref[0]

    def update(scores, pv_fn):
        mx = scores[0]
        for s in scores[1:]:
            mx = jnp.maximum(mx, s)
        m = m_scr[...]
        m_new = jnp.maximum(m, jnp.max(mx, axis=-1, keepdims=True))
        alpha = jnp.exp2(m - m_new)
        ps = [jnp.exp2(s - m_new) for s in scores]
        tot = ps[0]
        for p in ps[1:]:
            tot = tot + p
        l_scr[...] = alpha * l_scr[...] + jnp.sum(tot, axis=-1, keepdims=True)
        acc_scr[...] = alpha * acc_scr[...] + pv_fn([p.astype(BF16) for p in ps])
        m_scr[...] = m_new

    car = car_scr[...]
    biases = []
    for j in range(pps):
        f = f_refs[j][...]
        biases.append(f[:, :ps] + car)
        car = car + f[:, ps:]
    car_scr[...] = car
    bias_rows = _dot3_lhs_exact(rep, jnp.concatenate(biases, axis=1))
    scores = [_dot(qt, k_refs[j][0].astype(BF16)) + bias_rows[:, j * ps:(j + 1) * ps] for j in range(pps)]

    def pv_pages(ps_bf):
        pv = _dot_nt(ps_bf[0], v_refs[0][0].astype(BF16))
        for j in range(1, pps):
            pv = pv + _dot_nt(ps_bf[j], v_refs[j][0].astype(BF16))
        return pv

    update(scores, pv_pages)

    @pl.when(step == n_steps - 1)
    def _():
        na = lax.broadcasted_iota(jnp.int32, (lq, lq), 0)
        nb = lax.broadcasted_iota(jnp.int32, (lq, lq), 1)
        c_new = _dot3_rhs_exact(fn_ref[0], (na <= nb).astype(BF16)) * LOG2E
        cj = _dot3_lhs_exact(rep, c_new)
        qi = lax.broadcasted_iota(jnp.int32, (hq, lq), 0) % lq
        kj = lax.broadcasted_iota(jnp.int32, (hq, lq), 1)
        s = jnp.where(kj <= qi, _dot_nt(qt, kn_ref[0]) - cj, NEG_BIG)
        update([s], lambda ps_bf: _dot(ps_bf[0], vn_ref[0]))
        acc = acc_scr[...] / l_scr[...]
        rh = lax.broadcasted_iota(jnp.int32, (hq, D_FOX), 0) // lq
        ch = lax.broadcasted_iota(jnp.int32, (hq, D_FOX), 1) // DH_FOX
        acc = jnp.where(rh == ch, acc, 0.0)
        out = acc[0:lq]
        for h in range(1, H_FOX):
            out = out + acc[h * lq:(h + 1) * lq]
        o_ref[0] = out.astype(BF16)


def _fox_sample(page_table, kt_pages, vt_pages, f_pages, qt_bd, k_new, v_new, f_new_t, pps):
    bd, n_pages = page_table.shape
    lq = k_new.shape[1]
    hq = H_FOX * lq
    n_steps = n_pages // pps
    ps = kt_pages.shape[2]

    def page_map(j):
        return lambda b, p, pt: (pt[b, n_pages - 1 - (p * pps + j)], 0, 0)

    def page_map2(j):
        return lambda b, p, pt: (pt[b, n_pages - 1 - (p * pps + j)], 0)

    per_b = lambda shp: pl.BlockSpec((1,) + shp, lambda b, p, pt: (b, 0, 0))
    in_specs = ([pl.BlockSpec((1, D_FOX, ps), page_map(j)) for j in range(pps)]
                + [pl.BlockSpec((1, D_FOX, ps), page_map(j)) for j in range(pps)]
                + [pl.BlockSpec((H_FOX, 2 * ps), page_map2(j)) for j in range(pps)]
                + [per_b((hq, D_FOX)), per_b((lq, D_FOX)), per_b((lq, D_FOX)), per_b((H_FOX, lq))])
    grid_spec = pltpu.PrefetchScalarGridSpec(
        num_scalar_prefetch=1,
        grid=(bd, n_steps),
        in_specs=in_specs,
        out_specs=per_b((lq, D_FOX)),
        scratch_shapes=[pltpu.VMEM((hq, 1), F32), pltpu.VMEM((hq, 1), F32),
                        pltpu.VMEM((hq, D_FOX), F32), pltpu.VMEM((H_FOX, ps), F32)],
    )
    return pl.pallas_call(
        functools.partial(_fox_sample_kernel, pps=pps, n_steps=n_steps, lq=lq, ps=ps),
        grid_spec=grid_spec,
        out_shape=jax.ShapeDtypeStruct((bd, lq, D_FOX), BF16),
        name="fox_sample",
        compiler_params=_cparams(("parallel", "arbitrary")),
    )(page_table, *([kt_pages] * pps), *([vt_pages] * pps), *([f_pages] * pps), qt_bd, k_new, v_new, f_new_t)


def _page_suffix_kernel(x_ref, o_ref):
    x = x_ref[...]
    ps = x.shape[1]
    ja = lax.broadcasted_iota(jnp.int32, (ps, ps), 0)
    jb = lax.broadcasted_iota(jnp.int32, (ps, ps), 1)
    o_ref[:, :ps] = _dot3_rhs_exact(x, (ja > jb).astype(BF16)) * LOG2E
    o_ref[:, ps:] = _dot3_rhs_exact(x, jnp.ones((ps, ps), BF16)) * LOG2E


def _page_suffix(lf_rows):
    n, ps = lf_rows.shape
    tr = _pick_tile(n, 4096)
    return pl.pallas_call(
        _page_suffix_kernel,
        grid=(n // tr,),
        in_specs=[pl.BlockSpec((tr, ps), lambda i: (i, 0))],
        out_specs=pl.BlockSpec((tr, 2 * ps), lambda i: (i, 0)),
        out_shape=jax.ShapeDtypeStruct((n, 2 * ps), F32),
        name="page_suffix",
        compiler_params=_cparams(("parallel",)),
    )(lf_rows)


def _ret_head(q, k, v, rg, r, dec, qd, kd, cd):
    mm = q.dtype
    inner = _dot_nt(q, k) * dec
    o = _dot(inner.astype(mm), v) + qd * _dot(q, r.astype(mm))
    r_new = cd * r + _dot_tn((k.astype(F32) * kd).astype(mm), v)
    mu = jnp.mean(o, axis=-1, keepdims=True)
    var = jnp.mean(jnp.square(o - mu), axis=-1, keepdims=True)
    y = (o - mu) * lax.rsqrt(var + EPS)
    return (y * jax.nn.silu(rg)).astype(BF16), r_new


def _ret_prompt_kernel(q_ref, k_ref, v_ref, rg_ref, dec_ref, qd_ref, kd_ref, cd_ref, y_ref, rout_ref, r_scr,
                       *, chunk, n_chunks):
    step = pl.program_id(1)

    @pl.when(step == 0)
    def _():
        r_scr[...] = jnp.zeros(r_scr.shape, F32)

    for c in range(n_chunks):
        rows = slice(c * chunk, (c + 1) * chunk)
        for h in range(H_RET):
            qk = slice(h * DK_RET, (h + 1) * DK_RET)
            vv = slice(h * DV_RET, (h + 1) * DV_RET)
            y, r_new = _ret_head(q_ref[rows, qk], k_ref[rows, qk], v_ref[rows, vv], rg_ref[rows, vv], r_scr[h],
                                 dec_ref[h], qd_ref[h], kd_ref[h], cd_ref[h])
            y_ref[rows, vv] = y
            r_scr[h] = r_new

    @pl.when(step == pl.num_programs(1) - 1)
    def _():
        rout_ref[0] = r_scr[...]


def _ret_tables(chunk):
    lg = jnp.log1p(-jnp.exp2(-5.0 - jnp.arange(H_RET, dtype=F32)))
    pos = jnp.arange(chunk, dtype=F32)
    diff = pos[:, None] - pos[None, :]
    dec = jnp.where(diff >= 0, jnp.exp(lg[:, None, None] * jnp.maximum(diff, 0.0)), 0.0)
    q_dec = jnp.exp(lg[:, None] * (pos + 1.0))
    k_dec = jnp.exp(lg[:, None] * (chunk - 1.0 - pos))
    chunk_dec = jnp.exp(lg * chunk)
    qd = jnp.broadcast_to(q_dec[:, :, None], (H_RET, chunk, DV_RET))
    kd = jnp.broadcast_to(k_dec[:, :, None], (H_RET, chunk, DK_RET))
    cd = jnp.broadcast_to(chunk_dec[:, None, None], (H_RET, DK_RET, DV_RET))
    return dec, qd, kd, cd


def _ret_prompt(qb, kb, vb, rg, batch, n_chunks):
    t = qb.shape[0]
    chunk = RET_CHUNK
    rows = chunk * n_chunks
    steps = t // batch // rows
    row = lambda n: pl.BlockSpec((rows, n), lambda b, i: (b * steps + i, 0))
    const = lambda shp: pl.BlockSpec(shp, lambda b, i: (0,) * len(shp))
    tables = _ret_tables(chunk)
    return pl.pallas_call(
        functools.partial(_ret_prompt_kernel, chunk=chunk, n_chunks=n_chunks),
        grid=(batch, steps),
        in_specs=[row(D_RET_QK), row(D_RET_QK), row(D_RET_V), row(D_RET_V)] + [const(tb.shape) for tb in tables],
        out_specs=[row(D_RET_V), pl.BlockSpec((1, H_RET, DK_RET, DV_RET), lambda b, i: (b, 0, 0, 0))],
        out_shape=[jax.ShapeDtypeStruct((t, D_RET_V), BF16),
                   jax.ShapeDtypeStruct((batch, H_RET, DK_RET, DV_RET), F32)],
        scratch_shapes=[pltpu.VMEM((H_RET, DK_RET, DV_RET), F32)],
        name="ret_prompt",
        compiler_params=_cparams(("parallel", "arbitrary")),
    )(qb, kb, vb, rg, *tables)


def _ret_sample_kernel(q_ref, k_ref, v_ref, rg_ref, r0_ref, dec_ref, qd_ref, kd_ref, cd_ref, y_ref, rout_ref,
                       *, chunk, n_seq):
    for s in range(n_seq):
        rows = slice(s * chunk, (s + 1) * chunk)
        for h in range(H_RET):
            qk = slice(h * DK_RET, (h + 1) * DK_RET)
            vv = slice(h * DV_RET, (h + 1) * DV_RET)
            y, r_new = _ret_head(q_ref[rows, qk].astype(F32), k_ref[rows, qk].astype(F32),
                                 v_ref[rows, vv].astype(F32), rg_ref[rows, vv], r0_ref[s, h],
                                 dec_ref[h], qd_ref[h], kd_ref[h], cd_ref[h])
            y_ref[rows, vv] = y
            rout_ref[s, h] = r_new


def _ret_sample(qb, kb, vb, rg, r0, chunk, n_seq):
    t = qb.shape[0]
    rows = chunk * n_seq
    row = lambda n: pl.BlockSpec((rows, n), lambda i: (i, 0))
    const = lambda shp: pl.BlockSpec(shp, lambda i: (0,) * len(shp))
    st = pl.BlockSpec((n_seq, H_RET, DK_RET, DV_RET), lambda i: (i, 0, 0, 0))
    tables = _ret_tables(chunk)
    return pl.pallas_call(
        functools.partial(_ret_sample_kernel, chunk=chunk, n_seq=n_seq),
        grid=(t // rows,),
        in_specs=[row(D_RET_QK), row(D_RET_QK), row(D_RET_V), row(D_RET_V), st] + [const(tb.shape) for tb in tables],
        out_specs=[row(D_RET_V), st],
        out_shape=[jax.ShapeDtypeStruct((t, D_RET_V), BF16), jax.ShapeDtypeStruct(r0.shape, F32)],
        name="ret_sample",
        compiler_params=_cparams(("parallel",)),
    )(qb, kb, vb, rg, r0, *tables)


def _route(lg):
    lane = lax.broadcasted_iota(jnp.int32, lg.shape, 1)
    big = jnp.int32(1 << 20)
    is_grp = lane < N_GROUPS
    mg = jnp.max(jnp.where(is_grp, lg, -jnp.inf), axis=-1, keepdims=True)
    g_idx = jnp.min(jnp.where(is_grp & (lg == mg), lane, big), axis=-1, keepdims=True)
    g_w = 1.0 / jnp.sum(jnp.where(is_grp, jnp.exp(lg - mg), 0.0), axis=-1, keepdims=True)
    lo = _R_EXP0 + g_idx * EXPERTS_PER_GROUP
    sel = (lane >= lo) & (lane < lo + EXPERTS_PER_GROUP)
    v1 = jnp.max(jnp.where(sel, lg, -jnp.inf), axis=-1, keepdims=True)
    i1 = jnp.min(jnp.where(sel & (lg == v1), lane, big), axis=-1, keepdims=True)
    sel2 = sel & (lane != i1)
    v2 = jnp.max(jnp.where(sel2, lg, -jnp.inf), axis=-1, keepdims=True)
    i2 = jnp.min(jnp.where(sel2 & (lg == v2), lane, big), axis=-1, keepdims=True)
    e2 = jnp.exp(v2 - v1)
    w1 = g_w / (1.0 + e2)
    w2 = g_w * e2 / (1.0 + e2)
    cmb = jnp.where(lane == i1, w1, 0.0) + jnp.where(lane == i2, w2, 0.0)
    return jnp.where(lane == _R_GID, g_idx.astype(F32), cmb)


def _mix_kernel(yf_ref, yr_ref, ga_ref, gb_ref, x_ref, g1_ref, sc_ref, sh_ref, gain_ref,
                wf_ref, wr_ref, wo_ref, wrh_ref, wrl_ref, br_ref,
                x1_ref, h2_ref, cmb_ref, *, fox_transposed):
    if fox_transposed:
        pf = _dot_tn(yf_ref[0], wf_ref[...])
    else:
        pf = _dot(yf_ref[...], wf_ref[...])
    pr = _dot(yr_ref[...], wr_ref[...])
    mixed = jax.nn.sigmoid(ga_ref[...]) * pf + jax.nn.sigmoid(gb_ref[...]) * pr
    x1 = x_ref[...] + g1_ref[0] * _dot(mixed.astype(BF16), wo_ref[...])
    x1_ref[...] = x1
    y = x1 * lax.rsqrt(jnp.mean(x1 * x1, axis=-1, keepdims=True) + EPS) * gain_ref[...]
    h2 = y * (1.0 + sc_ref[0]) + sh_ref[0]
    h2_ref[...] = h2
    h_hi = h2.astype(BF16)
    h_lo = (h2 - h_hi.astype(F32)).astype(BF16)
    lg = _dot(h_hi, wrh_ref[...]) + (_dot(h_hi, wrl_ref[...]) + _dot(h_lo, wrh_ref[...])) + br_ref[...]
    cmb_ref[...] = _route(lg)


def _mix(yf, yr, ga, gb, x, g1, sc2, sh2, gain, wf, wr, wo, wrh, wrl, br, tm, tiles_per_batch, fox_transposed):
    t = x.shape[0]
    row = lambda n: pl.BlockSpec((tm, n), lambda i: (i, 0))
    const = lambda shp: pl.BlockSpec(shp, lambda i: (0,) * len(shp))
    if fox_transposed:
        yf_spec = pl.BlockSpec((1, D_FOX, tm), lambda i: (i // tiles_per_batch, 0, i % tiles_per_batch))
    else:
        yf_spec = row(D_FOX)
    ms = lambda m: _mod_spec(m, tiles_per_batch)
    return pl.pallas_call(
        functools.partial(_mix_kernel, fox_transposed=fox_transposed),
        grid=(t // tm,),
        in_specs=[yf_spec, row(D_RET_V), row(D_MODEL), row(D_MODEL), row(D_MODEL), ms(g1), ms(sc2), ms(sh2),
                  const((1, D_MODEL)), const((D_FOX, D_MODEL)), const((D_RET_V, D_MODEL)),
                  const((D_MODEL, D_MODEL)), const((D_MODEL, LANES)), const((D_MODEL, LANES)), const((1, LANES))],
        out_specs=[row(D_MODEL), row(D_MODEL), row(LANES)],
        out_shape=[jax.ShapeDtypeStruct((t, D_MODEL), F32), jax.ShapeDtypeStruct((t, D_MODEL), F32),
                   jax.ShapeDtypeStruct((t, LANES), F32)],
        name="mix_route",
        compiler_params=_cparams(("parallel",)),
    )(yf, yr, ga, gb, x, g1, sc2, sh2, gain, wf, wr, wo, wrh, wrl, br)


def _moe_plan_kernel(cmb_ref, pos_ref, meta_ref, p_scr, *, rb):
    tm = cmb_ref.shape[0]
    blk = min(256, tm)
    lane = lax.broadcasted_iota(jnp.int32, (blk, LANES), 1).astype(F32)
    ra = lax.broadcasted_iota(jnp.int32, (blk, blk), 0)
    rbi = lax.broadcasted_iota(jnp.int32, (blk, blk), 1)
    before = (rbi < ra).astype(BF16)
    run = jnp.zeros((1, LANES), F32)
    hots = []
    for b in range(tm // blk):
        gid = cmb_ref[b * blk:(b + 1) * blk, _R_GID:_R_GID + 1]
        hot = jnp.where(lane == gid, 1.0, 0.0)
        hots.append(hot)
        p_scr[b * blk:(b + 1) * blk, :] = _dot(before, hot.astype(BF16)) + run
        run = run + jnp.sum(hot, axis=0, keepdims=True)
    n_blocks = jnp.floor((run + (rb - 1)) * (1.0 / rb))
    la = lax.broadcasted_iota(jnp.int32, (LANES, LANES), 0)
    lb = lax.broadcasted_iota(jnp.int32, (LANES, LANES), 1)
    start = _dot(n_blocks.astype(BF16), (la < lb).astype(BF16)) * rb
    for b in range(tm // blk):
        pos = jnp.sum(hots[b] * (p_scr[b * blk:(b + 1) * blk, :] + start), axis=-1, keepdims=True)
        pos_ref[b * blk:(b + 1) * blk, :] = jnp.broadcast_to(pos, (blk, LANES))
    row = lax.broadcasted_iota(jnp.int32, (8, LANES), 0)
    meta_ref[0] = jnp.where(row == 0, n_blocks, jnp.where(row == 1, start, 0.0))


def _moe_plan(cmb, tm, rb):
    t = cmb.shape[0]
    return pl.pallas_call(
        functools.partial(_moe_plan_kernel, rb=rb),
        grid=(t // tm,),
        in_specs=[pl.BlockSpec((tm, LANES), lambda i: (i, 0))],
        out_specs=[pl.BlockSpec((tm, LANES), lambda i: (i, 0)), pl.BlockSpec((1, 8, LANES), lambda i: (i, 0, 0))],
        out_shape=[jax.ShapeDtypeStruct((t, LANES), F32), jax.ShapeDtypeStruct((t // tm, 8, LANES), F32)],
        scratch_shapes=[pltpu.VMEM((tm, LANES), F32)],
        name="moe_plan",
        compiler_params=_cparams(("parallel",)),
    )(cmb)


def _moe_kernel(meta_ref, pos_ref, hp_ref, cmb_ref, x1_ref, g2_ref, gain_ref, wg_ref, wu_ref, wd_ref, y_ref,
                xs_scr, cs_scr, ys_scr, *, rb, big, n_split):
    i, g, hf = pl.program_id(0), pl.program_id(1), pl.program_id(2)
    tm = hp_ref.shape[0]
    per_split = EXPERTS_PER_GROUP // n_split

    @pl.when((g == 0) & (hf == 0))
    def _():
        xs_scr[...] = jnp.zeros(xs_scr.shape, F32)
        cs_scr[...] = jnp.zeros(cs_scr.shape, F32)

        def move(t, c):
            p = pos_ref[0, 0, t]
            xs_scr[pl.ds(p, 1), :] = hp_ref[pl.ds(t, 1), :]
            cs_scr[pl.ds(p, 1), :] = cmb_ref[pl.ds(t, 1), :]
            return c

        lax.fori_loop(0, tm, move, 0, unroll=8)

    def run_rows(r0, rows):
        lane = lax.broadcasted_iota(jnp.int32, (rows, LANES), 1)
        x = xs_scr[pl.ds(r0, rows), :].astype(BF16)
        cw = cs_scr[pl.ds(r0, rows), :]
        hid = []
        for e in range(per_split):
            a = _dot(x, wg_ref[0, e])
            u = _dot(x, wu_ref[0, e])
            col = _R_EXP0 + g * EXPERTS_PER_GROUP + hf * per_split + e
            w = jnp.sum(jnp.where(lane == col, cw, 0.0), axis=-1, keepdims=True)
            hid.append((jax.nn.silu(a) * u * w).astype(BF16))
        y = _dot(jnp.concatenate(hid, axis=1), wd_ref[0])

        @pl.when(hf == 0)
        def _():
            ys_scr[pl.ds(r0, rows), :] = y

        @pl.when(hf != 0)
        def _():
            ys_scr[pl.ds(r0, rows), :] += y

    n_blocks = meta_ref[i, g]
    start = meta_ref[i, N_GROUPS + g]
    n_big = n_blocks // big

    def big_chunk(b, c):
        run_rows(pl.multiple_of(start + b * (big * rb), rb), big * rb)
        return c

    def small_chunk(b, c):
        run_rows(pl.multiple_of(start + b * rb, rb), rb)
        return c

    lax.fori_loop(0, n_big, big_chunk, 0)
    lax.fori_loop(n_big * big, n_blocks, small_chunk, 0)

    @pl.when((g == N_GROUPS - 1) & (hf == n_split - 1))
    def _():
        def move_back(t, c):
            y_ref[pl.ds(t, 1), :] = ys_scr[pl.ds(pos_ref[0, 0, t], 1), :]
            return c

        lax.fori_loop(0, tm, move_back, 0, unroll=8)
        x2 = x1_ref[...] + g2_ref[0] * y_ref[...]
        y_ref[...] = x2 * lax.rsqrt(jnp.mean(x2 * x2, axis=-1, keepdims=True) + EPS) * gain_ref[...]


def _moe(h2p, cmb, x1, g2, gain, wg, wu, wd, tm, tiles_per_batch):
    t = h2p.shape[0]
    rb = MOE_ROW_BLOCK
    n_split = wg.shape[0] // N_GROUPS
    n_tiles = t // tm
    rows_max = tm + N_GROUPS * rb
    pos_f, meta_f = _moe_plan(cmb, tm, rb)
    pos = pos_f[:, 0].astype(jnp.int32).reshape(n_tiles, 1, tm)
    meta = jnp.concatenate([meta_f[:, 0, :N_GROUPS], meta_f[:, 1, :N_GROUPS]], axis=1).astype(jnp.int32)

    row = lambda n: pl.BlockSpec((tm, n), lambda i, g, s, m: (i, 0))
    r = g2.shape[1]
    if tiles_per_batch is None:
        g2_spec = pl.BlockSpec((1, r, D_MODEL), lambda i, g, s, m: (i, 0, 0))
    else:
        g2_spec = pl.BlockSpec((1, r, D_MODEL), lambda i, g, s, m: (i // tiles_per_batch, 0, 0))
    wmap = lambda i, g, s, m: (g * n_split + s, 0, 0, 0)
    grid_spec = pltpu.PrefetchScalarGridSpec(
        num_scalar_prefetch=1,
        grid=(n_tiles, N_GROUPS, n_split),
        in_specs=[pl.BlockSpec((1, 1, tm), lambda i, g, s, m: (i, 0, 0), memory_space=pltpu.SMEM),
                  row(D_MODEL), row(LANES), row(D_MODEL), g2_spec,
                  pl.BlockSpec((1, D_MODEL), lambda i, g, s, m: (0, 0)),
                  pl.BlockSpec((1,) + wg.shape[1:], wmap),
                  pl.BlockSpec((1,) + wu.shape[1:], wmap),
                  pl.BlockSpec((1,) + wd.shape[1:], lambda i, g, s, m: (g * n_split + s, 0, 0))],
        out_specs=row(D_MODEL),
        scratch_shapes=[pltpu.VMEM((rows_max, D_MODEL), F32), pltpu.VMEM((rows_max, LANES), F32),
                        pltpu.VMEM((rows_max, D_MODEL), F32)],
    )
    return pl.pallas_call(
        functools.partial(_moe_kernel, rb=rb, big=MOE_BIG_CHUNK, n_split=n_split),
        grid_spec=grid_spec,
        out_shape=jax.ShapeDtypeStruct((t, D_MODEL), F32),
        name="moe_final",
        compiler_params=_cparams(("parallel", "arbitrary", "arbitrary")),
    )(meta, pos, h2p, cmb, x1, g2, gain, wg, wu, wd)


def _rope_tables(pos):
    half = DK_RET // 2
    inv = ROPE_BASE ** (-jnp.arange(half, dtype=F32) / half)
    ang = pos.astype(F32)[:, None] * inv[None, :]
    cos, sin = jnp.cos(ang), jnp.sin(ang)
    zero = jnp.zeros_like(sin)
    tile = lambda a, b: jnp.tile(jnp.concatenate([a, b], axis=-1), (1, H_RET))
    return tile(cos, cos), tile(-sin, zero), tile(zero, sin)


def _pick_tile(n, target):
    t = min(n, target)
    while n % t:
        t //= 2
    return t


def kernel(x_prompt, x_sample, cache_k, cache_v, cache_logf, state_ret, page_table, c_prompt, c_sample, w_ada, b_ada, norm_mix, norm_ffn, w_in, b_fgt, w_br_fox, w_br_ret, w_out, w_route_group, b_route_group, w_route_expert, b_route_expert, w_exp_gate, w_exp_up, w_exp_down, norm_final):
    depth = w_ada.shape[0]
    assert depth == 1
    bp, seq, d = x_prompt.shape
    bd, lq, _ = x_sample.shape
    n_pool, page_size = cache_k.shape[1], cache_k.shape[2]
    n_pages = page_table.shape[1]
    past_len = n_pages * page_size
    tp, ts = bp * seq, bd * lq

    wl = w_in[0]
    w_perm = jnp.concatenate(
        [wl[:, 0:1536], wl[:, 1544:5128], wl[:, 1536:1544], jnp.zeros((d, LANES - H_FOX), F32)], axis=1).astype(BF16)
    b_fgt_pad = jnp.pad(b_fgt[0], (0, LANES - H_FOX)).reshape(1, LANES)
    gain_mix, gain_ffn, gain_fin = norm_mix[0].reshape(1, d), norm_ffn[0].reshape(1, d), norm_final.reshape(1, d)
    wf, wr, wo = w_br_fox[0].astype(BF16), w_br_ret[0].astype(BF16), w_out[0].astype(BF16)
    w_rt = jnp.pad(jnp.concatenate([w_route_group[0], w_route_expert[0]], axis=1),
                   ((0, 0), (0, LANES - N_GROUPS - N_EXPERTS)))
    wrh = w_rt.astype(BF16)
    wrl = (w_rt - wrh.astype(F32)).astype(BF16)
    b_rt = jnp.pad(jnp.concatenate([b_route_group[0], b_route_expert[0]]), (0, LANES - N_GROUPS - N_EXPERTS))
    b_rt = b_rt.reshape(1, LANES)
    per_split = EXPERTS_PER_GROUP // MOE_SPLIT
    wg = w_exp_gate[0].astype(BF16).reshape(N_GROUPS * MOE_SPLIT, per_split, d, D_EXPERT)
    wu = w_exp_up[0].astype(BF16).reshape(N_GROUPS * MOE_SPLIT, per_split, d, D_EXPERT)
    wd = w_exp_down[0].astype(BF16).reshape(N_GROUPS * MOE_SPLIT, per_split * D_EXPERT, d)

    n_c = bp + bd
    n_c_pad = -(-n_c // 8) * 8
    c_all = jnp.pad(jnp.concatenate([c_prompt, c_sample], axis=0), ((0, n_c_pad - n_c), (0, 0)))
    mod = _ada_mod(c_all, w_ada[0], b_ada[0])
    mods = [mod[:, i * d:(i + 1) * d] for i in range(6)]
    tm_p = _pick_tile(seq, 512)
    tm_s = _pick_tile(ts, 512)
    mod_p = [m[:bp].reshape(bp, 1, d) for m in mods]
    mod_s = [jnp.repeat(m[bp:n_c], lq, axis=0).reshape(ts // tm_s, tm_s, d) for m in mods]

    xp = x_prompt.reshape(tp, d)
    tpb = seq // tm_p
    cos_p, sa_p, sb_p = _rope_tables(jnp.arange(seq))
    (qt, k_wide, kt, vt, vt_chunks, lf, qb, kb, vb, rg, ga, gb) = _in_proj(
        xp, mod_p[1], mod_p[0], gain_mix, w_perm, b_fgt_pad, cos_p, sa_p, sb_p, tm_p, tpb, tpb, True)
    logf_p = lf[:, :H_FOX].reshape(bp, seq, H_FOX)

    lf_rows = logf_p.transpose(0, 2, 1).reshape(bp * H_FOX, seq // LANES, LANES)
    c3 = jnp.stack([c.reshape(bp, H_FOX, seq) for c in _cumsum_split(lf_rows)], axis=2)
    c_rows = jnp.concatenate([c3, jnp.ones((bp, H_FOX, 3, seq), BF16),
                              jnp.zeros((bp, H_FOX, _C_ROWS - 6, seq), BF16)], axis=2)
    k_c = jnp.pad(-c3.transpose(0, 1, 3, 2), ((0, 0), (0, 0), (0, 0), (DH_FOX + 3, LANES - DH_FOX - 6)))
    tq = _pick_tile(seq, 512)
    yf_t = _fox_prompt(qt, c_rows, k_wide + k_c, vt_chunks, tq, 4).reshape(bp, D_FOX, seq)

    n_chunks = _pick_tile(seq // RET_CHUNK, 4)
    yr_p, rstate_p = _ret_prompt(qb, kb, vb, rg, bp, n_chunks)

    x1_p, h2_p, cmb_p = _mix(yf_t, yr_p, ga, gb, xp, mod_p[2], mod_p[4], mod_p[3], gain_ffn,
                             wf, wr, wo, wrh, wrl, b_rt, tm_p, tpb, True)
    tm_mp = _pick_tile(seq, MOE_TILE)
    y_p = _moe(h2_p, cmb_p, x1_p, mod_p[5], gain_fin, wg, wu, wd, tm_mp, seq // tm_mp)

    xs = x_sample.reshape(ts, d)
    cos_s, sa_s, sb_s = [jnp.tile(tb, (tm_s // lq, 1)) for tb in _rope_tables(past_len + jnp.arange(lq))]
    (qa_s, ka_s, va_s, lf_s, qb_s, kb_s, vb_s, rg_s, ga_s, gb_s) = _in_proj(
        xs, mod_s[1], mod_s[0], gain_mix, w_perm, b_fgt_pad, cos_s, sa_s, sb_s, tm_s, None, 1, False)
    logf_s = lf_s[:, :H_FOX].reshape(bd, lq, H_FOX)

    q4 = qa_s.reshape(bd, lq, H_FOX, DH_FOX).transpose(0, 2, 1, 3)
    eye = jnp.eye(H_FOX, dtype=BF16)
    qt_bd = (q4[:, :, :, None, :] * eye[None, :, None, :, None]).reshape(bd, H_FOX * lq, D_FOX)
    pps = _pick_tile(n_pages, 32)
    kt_pages = cache_k[0].transpose(0, 2, 3, 1).reshape(n_pool, D_FOX, page_size)
    vt_pages = cache_v[0].transpose(0, 2, 3, 1).reshape(n_pool, D_FOX, page_size)
    f_pages = _page_suffix(cache_logf[0].transpose(0, 2, 1).reshape(n_pool * H_FOX, page_size))
    yf_s = _fox_sample(page_table, kt_pages, vt_pages, f_pages,
                       qt_bd, ka_s.reshape(bd, lq, D_FOX).astype(BF16), va_s.reshape(bd, lq, D_FOX).astype(BF16),
                       logf_s.transpose(0, 2, 1), pps).reshape(ts, D_FOX)

    n_seq = _pick_tile(bd, 16)
    yr_s, rstate_s = _ret_sample(qb_s, kb_s, vb_s, rg_s, state_ret[0], lq, n_seq)

    x1_s, h2_s, cmb_s = _mix(yf_s, yr_s, ga_s, gb_s, xs, mod_s[2], mod_s[4], mod_s[3], gain_ffn,
                             wf, wr, wo, wrh, wrl, b_rt, tm_s, None, False)
    tm_ms = _pick_tile(ts, MOE_TILE)
    y_s = _moe(h2_s, cmb_s, x1_s, mod_s[5].reshape(ts // tm_ms, tm_ms, d), gain_fin, wg, wu, wd, tm_ms, None)

    return (y_p.reshape(bp, seq, d), y_s.reshape(bd, lq, d),
            kt.transpose(0, 3, 1, 2)[None], vt.transpose(0, 3, 1, 2)[None],
            logf_p[None], rstate_p[None],
            ka_s.reshape(1, bd, lq, H_FOX, DH_FOX), va_s.reshape(1, bd, lq, H_FOX, DH_FOX),
            logf_s[None], rstate_s[None])
```

```python
import functools

import numpy as np
import jax
import jax.numpy as jnp
from jax import lax
from jax.experimental import pallas as pl
from jax.experimental.pallas import tpu as pltpu

F32 = jnp.float32
BF16 = jnp.bfloat16

D_MODEL = 1024
H_FOX, DH_FOX = 8, 64
D_FOX = H_FOX * DH_FOX
H_RET, DK_RET, DV_RET = 4, 64, 128
D_RET_QK = H_RET * DK_RET
D_RET_V = H_RET * DV_RET
RET_CHUNK = 128
ROPE_BASE = 10000.0
N_GROUPS, EXPERTS_PER_GROUP, D_EXPERT = 4, 8, 256
N_EXPERTS = N_GROUPS * EXPERTS_PER_GROUP
EPS = 1e-6
LOG2E = 1.4426950408889634
NEG_BIG = -1e30
LANES = 128
KEY_STRIP = 256
V_ROWS = DH_FOX + 16

_C_QA, _C_KA, _C_VA = 0, 512, 1024
_C_QB, _C_KB, _C_VB, _C_RG, _C_GA, _C_GB, _C_FA, _C_END = 1536, 1792, 2048, 2560, 3072, 4096, 5120, 5248
_R_EXP0 = N_GROUPS
_R_GID = LANES - 1
MOE_TILE = 1024
MOE_SPLIT = 2
MOE_BIG_CHUNK = 4
MOE_ROW_BLOCK = 64

VMEM_LIMIT = 56 * 1024 * 1024


def _cparams(sem, vmem=VMEM_LIMIT, flags=None):
    return pltpu.CompilerParams(dimension_semantics=sem, vmem_limit_bytes=vmem, flags=flags)


def _split3(x):
    hi = x.astype(BF16)
    r1 = x - hi.astype(F32)
    mid = r1.astype(BF16)
    lo = (r1 - mid.astype(F32)).astype(BF16)
    return hi, mid, lo


def _dot(a, b):
    return jnp.dot(a, b, preferred_element_type=F32)


def _dot_nt(a, b):
    return lax.dot_general(a, b, (((1,), (1,)), ((), ())), preferred_element_type=F32)


def _dot_tn(a, b):
    return lax.dot_general(a, b, (((0,), (0,)), ((), ())), preferred_element_type=F32)


def _dot3_rhs_exact(x, m):
    return sum(_dot(p, m) for p in _split3(x))


def _dot3_lhs_exact(m, x):
    return sum(_dot(m, p) for p in _split3(x))


def _ada_kernel(c_ref, w_ref, b_ref, o_ref):
    a = jax.nn.silu(c_ref[...]).astype(BF16)
    o_ref[...] = _dot(a, w_ref[...].astype(BF16)) + b_ref[...]


def _ada_mod(c_all, w_ada, b_ada):
    rows = c_all.shape[0]
    n = w_ada.shape[1]
    tn = 1536
    return pl.pallas_call(
        _ada_kernel,
        grid=(n // tn,),
        in_specs=[pl.BlockSpec((rows, D_MODEL), lambda j: (0, 0)),
                  pl.BlockSpec((D_MODEL, tn), lambda j: (0, j)),
                  pl.BlockSpec((1, tn), lambda j: (0, j))],
        out_specs=pl.BlockSpec((rows, tn), lambda j: (0, j)),
        out_shape=jax.ShapeDtypeStruct((rows, n), F32),
        name="ada_mod",
        compiler_params=_cparams(("parallel",)),
    )(c_all, w_ada, b_ada.reshape(1, n))


def _inproj_kernel(x_ref, sc_ref, sh_ref, g_ref, w_ref, bf_ref, cos_ref, sa_ref, sb_ref, *out_refs, head_major):
    x = x_ref[...]
    tm = x.shape[0]
    y = x * lax.rsqrt(jnp.mean(x * x, axis=-1, keepdims=True) + EPS) * g_ref[...]
    h = (y * (1.0 + sc_ref[0]) + sh_ref[0]).astype(BF16)

    def proj(lo, hi):
        return _dot(h, w_ref[:, lo:hi])

    cos, sa, sb = cos_ref[...], sa_ref[...], sb_ref[...]

    def rope(t):
        return t * cos + pltpu.roll(t, D_RET_QK - DK_RET // 2, 1) * sa + pltpu.roll(t, DK_RET // 2, 1) * sb

    q = proj(_C_QA, _C_KA) * (DH_FOX ** -0.5 * LOG2E)
    k = proj(_C_KA, _C_VA)
    v = proj(_C_VA, _C_QB)
    if head_major:
        qt_ref, kw_ref, kt_ref, vt_ref, vc_ref = out_refs[:5]
        rest, t_scr = out_refs[5:-1], out_refs[-1]

        def transposed(a):
            t_scr[...] = a
            return t_scr[...].T

        qt_ref[0] = transposed(q).astype(BF16).reshape(H_FOX, DH_FOX, tm)
        kt_ref[0] = transposed(k).reshape(H_FOX, DH_FOX, tm)
        vt = transposed(v)
        vt_ref[0] = vt.reshape(H_FOX, DH_FOX, tm)
        vtb = vt.astype(BF16)
        lane = lax.broadcasted_iota(jnp.int32, (tm, LANES), 1)
        ones = jnp.where((lane >= DH_FOX) & (lane < DH_FOX + 3), 1.0, 0.0)
        srow = lax.broadcasted_iota(jnp.int32, (V_ROWS - DH_FOX, KEY_STRIP), 0)
        one_rows = jnp.where(srow == 0, 1.0, 0.0).astype(BF16)
        for hh in range(H_FOX):
            pair = k[:, (hh // 2) * LANES:(hh // 2 + 1) * LANES]
            if hh % 2:
                pair = pltpu.roll(pair, DH_FOX, 1)
            kw_ref[0, hh] = jnp.where(lane < DH_FOX, pair, ones).astype(BF16)
            for c in range(tm // KEY_STRIP):
                vc_ref[0, hh, c, :DH_FOX] = vtb[hh * DH_FOX:(hh + 1) * DH_FOX, c * KEY_STRIP:(c + 1) * KEY_STRIP]
                vc_ref[0, hh, c, DH_FOX:] = one_rows
    else:
        q_ref, k_ref, v_ref = out_refs[:3]
        rest = out_refs[3:]
        q_ref[...] = q.astype(BF16)
        k_ref[...] = k
        v_ref[...] = v
    lf_ref, qb_ref, kb_ref, vb_ref, rg_ref, ga_ref, gb_ref = rest
    qb_ref[...] = rope(proj(_C_QB, _C_KB)).astype(BF16)
    kb_ref[...] = (rope(proj(_C_KB, _C_VB)) * DK_RET ** -0.5).astype(BF16)
    vb_ref[...] = proj(_C_VB, _C_RG).astype(BF16)
    rg_ref[...] = proj(_C_RG, _C_GA)
    ga_ref[...] = proj(_C_GA, _C_GB)
    gb_ref[...] = proj(_C_GB, _C_FA)
    lf_ref[...] = jax.nn.log_sigmoid(proj(_C_FA, _C_END) + bf_ref[...])


def _mod_spec(mod, tiles_per_batch):
    r = mod.shape[1]
    if tiles_per_batch is None:
        return pl.BlockSpec((1, r, D_MODEL), lambda i: (i, 0, 0))
    return pl.BlockSpec((1, r, D_MODEL), lambda i: (i // tiles_per_batch, 0, 0))


def _in_proj(x, sc, sh, gain, w_perm, b_fgt_pad, cos, sa, sb, tm, tiles_per_batch, rope_tiles, head_major):
    t = x.shape[0]
    row = lambda n: pl.BlockSpec((tm, n), lambda i: (i, 0))
    const = lambda shp: pl.BlockSpec(shp, lambda i: (0,) * len(shp), pipeline_mode=pl.Buffered(1))
    rope_spec = pl.BlockSpec((tm, D_RET_QK), lambda i: (i % rope_tiles, 0))
    outs = [(LANES, F32), (D_RET_QK, BF16), (D_RET_QK, BF16), (D_RET_V, BF16), (D_RET_V, F32),
            (D_MODEL, F32), (D_MODEL, F32)]
    if head_major:
        tpb = tiles_per_batch
        nb, seq = t // (tpb * tm), tpb * tm
        tspec = pl.BlockSpec((1, H_FOX, DH_FOX, tm), lambda i: (i // tpb, 0, 0, i % tpb))
        head_specs = [tspec, pl.BlockSpec((1, H_FOX, tm, LANES), lambda i: (i // tpb, 0, i % tpb, 0)), tspec, tspec,
                      pl.BlockSpec((1, H_FOX, tm // KEY_STRIP, V_ROWS, KEY_STRIP),
                                   lambda i: (i // tpb, 0, i % tpb, 0, 0))]
        head_shapes = [jax.ShapeDtypeStruct((nb, H_FOX, DH_FOX, seq), BF16),
                       jax.ShapeDtypeStruct((nb, H_FOX, seq, LANES), BF16),
                       jax.ShapeDtypeStruct((nb, H_FOX, DH_FOX, seq), F32),
                       jax.ShapeDtypeStruct((nb, H_FOX, DH_FOX, seq), F32),
                       jax.ShapeDtypeStruct((nb, H_FOX, seq // KEY_STRIP, V_ROWS, KEY_STRIP), BF16)]
    else:
        head_specs = [row(D_FOX)] * 3
        head_shapes = [jax.ShapeDtypeStruct((t, D_FOX), dt) for dt in (BF16, F32, F32)]
    return pl.pallas_call(
        functools.partial(_inproj_kernel, head_major=head_major),
        grid=(t // tm,),
        in_specs=[row(D_MODEL), _mod_spec(sc, tiles_per_batch), _mod_spec(sh, tiles_per_batch),
                  const((1, D_MODEL)), const((D_MODEL, _C_END)), const((1, LANES)),
                  rope_spec, rope_spec, rope_spec],
        out_specs=head_specs + [row(n) for n, _ in outs],
        out_shape=head_shapes + [jax.ShapeDtypeStruct((t, n), dt) for n, dt in outs],
        scratch_shapes=[pltpu.VMEM((tm, D_FOX), F32)] if head_major else [],
        name="in_proj",
        compiler_params=_cparams(("parallel",)),
    )(x, sc, sh, gain, w_perm, b_fgt_pad, cos, sa, sb)


def _cumsum_kernel(x_ref, hi_ref, mid_ref, lo_ref):
    x = x_ref[0]
    r = x.shape[0]
    a = lax.broadcasted_iota(jnp.int32, (LANES, LANES), 0)
    b = lax.broadcasted_iota(jnp.int32, (LANES, LANES), 1)
    incl = (a <= b).astype(BF16)
    y = _dot3_rhs_exact(x, incl)
    tot = jnp.broadcast_to(y[:, LANES - 1:LANES], (r, LANES))
    ra = lax.broadcasted_iota(jnp.int32, (r, r), 0)
    rb = lax.broadcasted_iota(jnp.int32, (r, r), 1)
    strict = (rb < ra).astype(BF16)
    c = (y + _dot3_lhs_exact(strict, tot)) * LOG2E
    hi, mid, lo = _split3(c)
    hi_ref[0] = hi
    mid_ref[0] = mid
    lo_ref[0] = lo


def _cumsum_split(lf_rows):
    n, r, _ = lf_rows.shape
    spec = pl.BlockSpec((1, r, LANES), lambda i: (i, 0, 0))
    return pl.pallas_call(
        _cumsum_kernel,
        grid=(n,),
        in_specs=[spec],
        out_specs=[spec] * 3,
        out_shape=[jax.ShapeDtypeStruct((n, r, LANES), BF16)] * 3,
        name="logf_cumsum",
        compiler_params=_cparams(("parallel",)),
    )(lf_rows)


_C_ROWS = 16


def _fox_prompt_kernel(q_ref, ca_ref, k_ref, v_ref, o_ref, m_scr, acc_scr, *, tq, ks):
    i = pl.program_id(2)
    n_heads = q_ref.shape[1]
    per = tq // ks
    qts = [jnp.concatenate([q_ref[0, g], ca_ref[0, g], jnp.zeros((LANES - DH_FOX - _C_ROWS, tq), BF16)], axis=0)
           for g in range(n_heads)]

    m_scr[...] = jnp.full(m_scr.shape, NEG_BIG, F32)
    acc_scr[...] = jnp.zeros(acc_scr.shape, F32)

    def strips(c0, masked):
        cs = [c0 + u for u in range(per)]
        scores = [[_dot(k_ref[0, g, pl.ds(pl.multiple_of(c * ks, ks), ks), :], qts[g]) for g in range(n_heads)]
                  for c in cs]
        for u, c in enumerate(cs):
            for g in range(n_heads):
                s = scores[u][g]
                if masked:
                    kpos = c * ks + lax.broadcasted_iota(jnp.int32, (ks, tq), 0)
                    qpos = i * tq + lax.broadcasted_iota(jnp.int32, (ks, tq), 1)
                    s = jnp.where(kpos <= qpos, s, NEG_BIG)
                m = m_scr[g]
                m_new = jnp.maximum(m, jnp.max(s, axis=0, keepdims=True))
                alpha = jnp.exp2(m - m_new)
                p = jnp.exp2(s - m_new)
                acc_scr[g] = alpha * acc_scr[g] + _dot(v_ref[0, g, c], p.astype(BF16))
                m_scr[g] = m_new

    def block(j, carry):
        strips(j * per, False)
        return carry

    lax.fori_loop(0, i, block, 0)
    strips(i * per, True)
    for g in range(n_heads):
        o_ref[0, g] = (acc_scr[g, :DH_FOX] / acc_scr[g, DH_FOX:DH_FOX + 1]).astype(BF16)


def _fox_prompt(qt, c_rows, k_aug, vt_chunks, tq, heads_per_step):
    b, h, _, s = qt.shape
    g = heads_per_step
    ks = vt_chunks.shape[-1]
    qmap = lambda bi, hi, i: (bi, hi, 0, i)
    resident = lambda shp: pl.BlockSpec(shp, lambda bi, hi, i: (bi, hi) + (0,) * (len(shp) - 2),
                                        pipeline_mode=pl.Buffered(1))
    return pl.pallas_call(
        functools.partial(_fox_prompt_kernel, tq=tq, ks=ks),
        grid=(b, h // g, s // tq),
        in_specs=[pl.BlockSpec((1, g, DH_FOX, tq), qmap),
                  pl.BlockSpec((1, g, _C_ROWS, tq), qmap),
                  resident((1, g, s, LANES)),
                  resident((1, g, s // ks, V_ROWS, ks))],
        out_specs=pl.BlockSpec((1, g, DH_FOX, tq), qmap),
        out_shape=jax.ShapeDtypeStruct((b, h, DH_FOX, s), BF16),
        scratch_shapes=[pltpu.VMEM((g, 1, tq), F32), pltpu.VMEM((g, V_ROWS, tq), F32)],
        name="fox_prompt",
        compiler_params=_cparams(("parallel", "parallel", "arbitrary")),
    )(qt, c_rows, k_aug, vt_chunks)


def _fox_sample_kernel(pt_ref, *refs, pps, n_steps, lq, ps):
    k_refs = refs[:pps]
    v_refs = refs[pps:2 * pps]
    f_refs = refs[2 * pps:3 * pps]
    qt_ref, kn_ref, vn_ref, fn_ref, o_ref, m_scr, l_scr, acc_scr, car_scr = refs[3 * pps:]
    del pt_ref
    hq = H_FOX * lq
    step = pl.program_id(1)

    row_h = lax.broadcasted_iota(jnp.int32, (hq, H_FOX), 0) // lq
    col_h = lax.broadcasted_iota(jnp.int32, (hq, H_FOX), 1)
    rep = (row_h == col_h).astype(BF16)

    @pl.when(step == 0)
    def _():
        m_scr[...] = jnp.full(m_scr.shape, NEG_BIG, F32)
        l_scr[...] = jnp.zeros(l_scr.shape, F32)
        acc_scr[...] = jnp.zeros(acc_scr.shape, F32)
        car_scr[...] = jnp.zeros(car_scr.shape, F32)

    qt = qt_ref[0]

    def update(scores, pv_fn):
        mx = scores[0]
        for s in scores[1:]:
            mx = jnp.maximum(mx, s)
        m = m_scr[...]
        m_new = jnp.maximum(m, jnp.max(mx, axis=-1, keepdims=True))
        alpha = jnp.exp2(m - m_new)
        ps = [jnp.exp2(s - m_new) for s in scores]
        tot = ps[0]
        for p in ps[1:]:
            tot = tot + p
        l_scr[...] = alpha * l_scr[...] + jnp.sum(tot, axis=-1, keepdims=True)
        acc_scr[...] = alpha * acc_scr[...] + pv_fn([p.astype(BF16) for p in ps])
        m_scr[...] = m_new

    car = car_scr[...]
    biases = []
    for j in range(pps):
        f = f_refs[j][...]
        biases.append(f[:, :ps] + car)
        car = car + f[:, ps:]
    car_scr[...] = car
    bias_rows = _dot3_lhs_exact(rep, jnp.concatenate(biases, axis=1))
    scores = [_dot(qt, k_refs[j][0].astype(BF16)) + bias_rows[:, j * ps:(j + 1) * ps] for j in range(pps)]

    def pv_pages(ps_bf):
        pv = _dot_nt(ps_bf[0], v_refs[0][0].astype(BF16))
        for j in range(1, pps):
            pv = pv + _dot_nt(ps_bf[j], v_refs[j][0].astype(BF16))
        return pv

    update(scores, pv_pages)

    @pl.when(step == n_steps - 1)
    def _():
        na = lax.broadcasted_iota(jnp.int32, (lq, lq), 0)
        nb = lax.broadcasted_iota(jnp.int32, (lq, lq), 1)
        c_new = _dot3_rhs_exact(fn_ref[0], (na <= nb).astype(BF16)) * LOG2E
        cj = _dot3_lhs_exact(rep, c_new)
        qi = lax.broadcasted_iota(jnp.int32, (hq, lq), 0) % lq
        kj = lax.broadcasted_iota(jnp.int32, (hq, lq), 1)
        s = jnp.where(kj <= qi, _dot_nt(qt, kn_ref[0]) - cj, NEG_BIG)
        update([s], lambda ps_bf: _dot(ps_bf[0], vn_ref[0]))
        acc = acc_scr[...] / l_scr[...]
        rh = lax.broadcasted_iota(jnp.int32, (hq, D_FOX), 0) // lq
        ch = lax.broadcasted_iota(jnp.int32, (hq, D_FOX), 1) // DH_FOX
        acc = jnp.where(rh == ch, acc, 0.0)
        out = acc[0:lq]
        for h in range(1, H_FOX):
            out = out + acc[h * lq:(h + 1) * lq]
        o_ref[0] = out.astype(BF16)


def _fox_sample(page_table, kt_pages, vt_pages, f_pages, qt_bd, k_new, v_new, f_new_t, pps):
    bd, n_pages = page_table.shape
    lq = k_new.shape[1]
    hq = H_FOX * lq
    n_steps = n_pages // pps
    ps = kt_pages.shape[2]

    def page_map(j):
        return lambda b, p, pt: (pt[b, n_pages - 1 - (p * pps + j)], 0, 0)

    def page_map2(j):
        return lambda b, p, pt: (pt[b, n_pages - 1 - (p * pps + j)], 0)

    per_b = lambda shp: pl.BlockSpec((1,) + shp, lambda b, p, pt: (b, 0, 0))
    in_specs = ([pl.BlockSpec((1, D_FOX, ps), page_map(j)) for j in range(pps)]
                + [pl.BlockSpec((1, D_FOX, ps), page_map(j)) for j in range(pps)]
                + [pl.BlockSpec((H_FOX, 2 * ps), page_map2(j)) for j in range(pps)]
                + [per_b((hq, D_FOX)), per_b((lq, D_FOX)), per_b((lq, D_FOX)), per_b((H_FOX, lq))])
    grid_spec = pltpu.PrefetchScalarGridSpec(
        num_scalar_prefetch=1,
        grid=(bd, n_steps),
        in_specs=in_specs,
        out_specs=per_b((lq, D_FOX)),
        scratch_shapes=[pltpu.VMEM((hq, 1), F32), pltpu.VMEM((hq, 1), F32),
                        pltpu.VMEM((hq, D_FOX), F32), pltpu.VMEM((H_FOX, ps), F32)],
    )
    return pl.pallas_call(
        functools.partial(_fox_sample_kernel, pps=pps, n_steps=n_steps, lq=lq, ps=ps),
        grid_spec=grid_spec,
        out_shape=jax.ShapeDtypeStruct((bd, lq, D_FOX), BF16),
        name="fox_sample",
        compiler_params=_cparams(("parallel", "arbitrary")),
    )(page_table, *([kt_pages] * pps), *([vt_pages] * pps), *([f_pages] * pps), qt_bd, k_new, v_new, f_new_t)


def _page_suffix_kernel(x_ref, o_ref):
    x = x_ref[...]
    ps = x.shape[1]
    ja = lax.broadcasted_iota(jnp.int32, (ps, ps), 0)
    jb = lax.broadcasted_iota(jnp.int32, (ps, ps), 1)
    o_ref[:, :ps] = _dot3_rhs_exact(x, (ja > jb).astype(BF16)) * LOG2E
    o_ref[:, ps:] = _dot3_rhs_exact(x, jnp.ones((ps, ps), BF16)) * LOG2E


def _page_suffix(lf_rows):
    n, ps = lf_rows.shape
    tr = _pick_tile(n, 4096)
    return pl.pallas_call(
        _page_suffix_kernel,
        grid=(n // tr,),
        in_specs=[pl.BlockSpec((tr, ps), lambda i: (i, 0))],
        out_specs=pl.BlockSpec((tr, 2 * ps), lambda i: (i, 0)),
        out_shape=jax.ShapeDtypeStruct((n, 2 * ps), F32),
        name="page_suffix",
        compiler_params=_cparams(("parallel",)),
    )(lf_rows)


def _ret_head(q, k, v, rg, r, dec, qd, kd, cd):
    mm = q.dtype
    inner = _dot_nt(q, k) * dec
    o = _dot(inner.astype(mm), v) + qd * _dot(q, r.astype(mm))
    r_new = cd * r + _dot_tn((k.astype(F32) * kd).astype(mm), v)
    mu = jnp.mean(o, axis=-1, keepdims=True)
    var = jnp.mean(jnp.square(o - mu), axis=-1, keepdims=True)
    y = (o - mu) * lax.rsqrt(var + EPS)
    return (y * jax.nn.silu(rg)).astype(BF16), r_new


def _ret_prompt_kernel(q_ref, k_ref, v_ref, rg_ref, dec_ref, qd_ref, kd_ref, cd_ref, y_ref, rout_ref, r_scr,
                       *, chunk, n_chunks):
    step = pl.program_id(1)

    @pl.when(step == 0)
    def _():
        r_scr[...] = jnp.zeros(r_scr.shape, F32)

    for c in range(n_chunks):
        rows = slice(c * chunk, (c + 1) * chunk)
        for h in range(H_RET):
            qk = slice(h * DK_RET, (h + 1) * DK_RET)
            vv = slice(h * DV_RET, (h + 1) * DV_RET)
            y, r_new = _ret_head(q_ref[rows, qk], k_ref[rows, qk], v_ref[rows, vv], rg_ref[rows, vv], r_scr[h],
                                 dec_ref[h], qd_ref[h], kd_ref[h], cd_ref[h])
            y_ref[rows, vv] = y
            r_scr[h] = r_new

    @pl.when(step == pl.num_programs(1) - 1)
    def _():
        rout_ref[0] = r_scr[...]


def _ret_tables(chunk):
    lg = jnp.log1p(-jnp.exp2(-5.0 - jnp.arange(H_RET, dtype=F32)))
    pos = jnp.arange(chunk, dtype=F32)
    diff = pos[:, None] - pos[None, :]
    dec = jnp.where(diff >= 0, jnp.exp(lg[:, None, None] * jnp.maximum(diff, 0.0)), 0.0)
    q_dec = jnp.exp(lg[:, None] * (pos + 1.0))
    k_dec = jnp.exp(lg[:, None] * (chunk - 1.0 - pos))
    chunk_dec = jnp.exp(lg * chunk)
    qd = jnp.broadcast_to(q_dec[:, :, None], (H_RET, chunk, DV_RET))
    kd = jnp.broadcast_to(k_dec[:, :, None], (H_RET, chunk, DK_RET))
    cd = jnp.broadcast_to(chunk_dec[:, None, None], (H_RET, DK_RET, DV_RET))
    return dec, qd, kd, cd


def _ret_prompt(qb, kb, vb, rg, batch, n_chunks):
    t = qb.shape[0]
    chunk = RET_CHUNK
    rows = chunk * n_chunks
    steps = t // batch // rows
    row = lambda n: pl.BlockSpec((rows, n), lambda b, i: (b * steps + i, 0))
    const = lambda shp: pl.BlockSpec(shp, lambda b, i: (0,) * len(shp))
    tables = _ret_tables(chunk)
    return pl.pallas_call(
        functools.partial(_ret_prompt_kernel, chunk=chunk, n_chunks=n_chunks),
        grid=(batch, steps),
        in_specs=[row(D_RET_QK), row(D_RET_QK), row(D_RET_V), row(D_RET_V)] + [const(tb.shape) for tb in tables],
        out_specs=[row(D_RET_V), pl.BlockSpec((1, H_RET, DK_RET, DV_RET), lambda b, i: (b, 0, 0, 0))],
        out_shape=[jax.ShapeDtypeStruct((t, D_RET_V), BF16),
                   jax.ShapeDtypeStruct((batch, H_RET, DK_RET, DV_RET), F32)],
        scratch_shapes=[pltpu.VMEM((H_RET, DK_RET, DV_RET), F32)],
        name="ret_prompt",
        compiler_params=_cparams(("parallel", "arbitrary")),
    )(qb, kb, vb, rg, *tables)


def _ret_sample_kernel(q_ref, k_ref, v_ref, rg_ref, r0_ref, dec_ref, qd_ref, kd_ref, cd_ref, y_ref, rout_ref,
                       *, chunk, n_seq):
    for s in range(n_seq):
        rows = slice(s * chunk, (s + 1) * chunk)
        for h in range(H_RET):
            qk = slice(h * DK_RET, (h + 1) * DK_RET)
            vv = slice(h * DV_RET, (h + 1) * DV_RET)
            y, r_new = _ret_head(q_ref[rows, qk].astype(F32), k_ref[rows, qk].astype(F32),
                                 v_ref[rows, vv].astype(F32), rg_ref[rows, vv], r0_ref[s, h],
                                 dec_ref[h], qd_ref[h], kd_ref[h], cd_ref[h])
            y_ref[rows, vv] = y
            rout_ref[s, h] = r_new


def _ret_sample(qb, kb, vb, rg, r0, chunk, n_seq):
    t = qb.shape[0]
    rows = chunk * n_seq
    row = lambda n: pl.BlockSpec((rows, n), lambda i: (i, 0))
    const = lambda shp: pl.BlockSpec(shp, lambda i: (0,) * len(shp))
    st = pl.BlockSpec((n_seq, H_RET, DK_RET, DV_RET), lambda i: (i, 0, 0, 0))
    tables = _ret_tables(chunk)
    return pl.pallas_call(
        functools.partial(_ret_sample_kernel, chunk=chunk, n_seq=n_seq),
        grid=(t // rows,),
        in_specs=[row(D_RET_QK), row(D_RET_QK), row(D_RET_V), row(D_RET_V), st] + [const(tb.shape) for tb in tables],
        out_specs=[row(D_RET_V), st],
        out_shape=[jax.ShapeDtypeStruct((t, D_RET_V), BF16), jax.ShapeDtypeStruct(r0.shape, F32)],
        name="ret_sample",
        compiler_params=_cparams(("parallel",)),
    )(qb, kb, vb, rg, r0, *tables)


def _route(lg):
    lane = lax.broadcasted_iota(jnp.int32, lg.shape, 1)
    big = jnp.int32(1 << 20)
    is_grp = lane < N_GROUPS
    mg = jnp.max(jnp.where(is_grp, lg, -jnp.inf), axis=-1, keepdims=True)
    g_idx = jnp.min(jnp.where(is_grp & (lg == mg), lane, big), axis=-1, keepdims=True)
    g_w = 1.0 / jnp.sum(jnp.where(is_grp, jnp.exp(lg - mg), 0.0), axis=-1, keepdims=True)
    lo = _R_EXP0 + g_idx * EXPERTS_PER_GROUP
    sel = (lane >= lo) & (lane < lo + EXPERTS_PER_GROUP)
    v1 = jnp.max(jnp.where(sel, lg, -jnp.inf), axis=-1, keepdims=True)
    i1 = jnp.min(jnp.where(sel & (lg == v1), lane, big), axis=-1, keepdims=True)
    sel2 = sel & (lane != i1)
    v2 = jnp.max(jnp.where(sel2, lg, -jnp.inf), axis=-1, keepdims=True)
    i2 = jnp.min(jnp.where(sel2 & (lg == v2), lane, big), axis=-1, keepdims=True)
    e2 = jnp.exp(v2 - v1)
    w1 = g_w / (1.0 + e2)
    w2 = g_w * e2 / (1.0 + e2)
    cmb = jnp.where(lane == i1, w1, 0.0) + jnp.where(lane == i2, w2, 0.0)
    return jnp.where(lane == _R_GID, g_idx.astype(F32), cmb)


def _mix_kernel(yf_ref, yr_ref, ga_ref, gb_ref, x_ref, g1_ref, sc_ref, sh_ref, gain_ref,
                wf_ref, wr_ref, wo_ref, wrh_ref, wrl_ref, br_ref,
                x1_ref, h2_ref, cmb_ref, *, fox_transposed):
    if fox_transposed:
        pf = _dot_tn(yf_ref[0], wf_ref[...])
    else:
        pf = _dot(yf_ref[...], wf_ref[...])
    pr = _dot(yr_ref[...], wr_ref[...])
    mixed = jax.nn.sigmoid(ga_ref[...]) * pf + jax.nn.sigmoid(gb_ref[...]) * pr
    x1 = x_ref[...] + g1_ref[0] * _dot(mixed.astype(BF16), wo_ref[...])
    x1_ref[...] = x1
    y = x1 * lax.rsqrt(jnp.mean(x1 * x1, axis=-1, keepdims=True) + EPS) * gain_ref[...]
    h2 = y * (1.0 + sc_ref[0]) + sh_ref[0]
    h2_ref[...] = h2
    h_hi = h2.astype(BF16)
    h_lo = (h2 - h_hi.astype(F32)).astype(BF16)
    lg = _dot(h_hi, wrh_ref[...]) + (_dot(h_hi, wrl_ref[...]) + _dot(h_lo, wrh_ref[...])) + br_ref[...]
    cmb_ref[...] = _route(lg)


def _mix(yf, yr, ga, gb, x, g1, sc2, sh2, gain, wf, wr, wo, wrh, wrl, br, tm, tiles_per_batch, fox_transposed):
    t = x.shape[0]
    row = lambda n: pl.BlockSpec((tm, n), lambda i: (i, 0))
    const = lambda shp: pl.BlockSpec(shp, lambda i: (0,) * len(shp))
    if fox_transposed:
        yf_spec = pl.BlockSpec((1, D_FOX, tm), lambda i: (i // tiles_per_batch, 0, i % tiles_per_batch))
    else:
        yf_spec = row(D_FOX)
    ms = lambda m: _mod_spec(m, tiles_per_batch)
    return pl.pallas_call(
        functools.partial(_mix_kernel, fox_transposed=fox_transposed),
        grid=(t // tm,),
        in_specs=[yf_spec, row(D_RET_V), row(D_MODEL), row(D_MODEL), row(D_MODEL), ms(g1), ms(sc2), ms(sh2),
                  const((1, D_MODEL)), const((D_FOX, D_MODEL)), const((D_RET_V, D_MODEL)),
                  const((D_MODEL, D_MODEL)), const((D_MODEL, LANES)), const((D_MODEL, LANES)), const((1, LANES))],
        out_specs=[row(D_MODEL), row(D_MODEL), row(LANES)],
        out_shape=[jax.ShapeDtypeStruct((t, D_MODEL), F32), jax.ShapeDtypeStruct((t, D_MODEL), F32),
                   jax.ShapeDtypeStruct((t, LANES), F32)],
        name="mix_route",
        compiler_params=_cparams(("parallel",)),
    )(yf, yr, ga, gb, x, g1, sc2, sh2, gain, wf, wr, wo, wrh, wrl, br)


def _moe_plan_kernel(cmb_ref, pos_ref, meta_ref, p_scr, *, rb):
    tm = cmb_ref.shape[0]
    blk = min(256, tm)
    lane = lax.broadcasted_iota(jnp.int32, (blk, LANES), 1).astype(F32)
    ra = lax.broadcasted_iota(jnp.int32, (blk, blk), 0)
    rbi = lax.broadcasted_iota(jnp.int32, (blk, blk), 1)
    before = (rbi < ra).astype(BF16)
    run = jnp.zeros((1, LANES), F32)
    hots = []
    for b in range(tm // blk):
        gid = cmb_ref[b * blk:(b + 1) * blk, _R_GID:_R_GID + 1]
        hot = jnp.where(lane == gid, 1.0, 0.0)
        hots.append(hot)
        p_scr[b * blk:(b + 1) * blk, :] = _dot(before, hot.astype(BF16)) + run
        run = run + jnp.sum(hot, axis=0, keepdims=True)
    n_blocks = jnp.floor((run + (rb - 1)) * (1.0 / rb))
    la = lax.broadcasted_iota(jnp.int32, (LANES, LANES), 0)
    lb = lax.broadcasted_iota(jnp.int32, (LANES, LANES), 1)
    start = _dot(n_blocks.astype(BF16), (la < lb).astype(BF16)) * rb
    for b in range(tm // blk):
        pos = jnp.sum(hots[b] * (p_scr[b * blk:(b + 1) * blk, :] + start), axis=-1, keepdims=True)
        pos_ref[b * blk:(b + 1) * blk, :] = jnp.broadcast_to(pos, (blk, LANES))
    row = lax.broadcasted_iota(jnp.int32, (8, LANES), 0)
    meta_ref[0] = jnp.where(row == 0, n_blocks, jnp.where(row == 1, start, 0.0))


def _moe_plan(cmb, tm, rb):
    t = cmb.shape[0]
    return pl.pallas_call(
        functools.partial(_moe_plan_kernel, rb=rb),
        grid=(t // tm,),
        in_specs=[pl.BlockSpec((tm, LANES), lambda i: (i, 0))],
        out_specs=[pl.BlockSpec((tm, LANES), lambda i: (i, 0)), pl.BlockSpec((1, 8, LANES), lambda i: (i, 0, 0))],
        out_shape=[jax.ShapeDtypeStruct((t, LANES), F32), jax.ShapeDtypeStruct((t // tm, 8, LANES), F32)],
        scratch_shapes=[pltpu.VMEM((tm, LANES), F32)],
        name="moe_plan",
        compiler_params=_cparams(("parallel",)),
    )(cmb)


def _moe_kernel(meta_ref, pos_ref, hp_ref, cmb_ref, x1_ref, g2_ref, gain_ref, wg_ref, wu_ref, wd_ref, y_ref,
                xs_scr, cs_scr, ys_scr, *, rb, big, n_split):
    i, g, hf = pl.program_id(0), pl.program_id(1), pl.program_id(2)
    tm = hp_ref.shape[0]
    per_split = EXPERTS_PER_GROUP // n_split

    @pl.when((g == 0) & (hf == 0))
    def _():
        xs_scr[...] = jnp.zeros(xs_scr.shape, F32)
        cs_scr[...] = jnp.zeros(cs_scr.shape, F32)

        def move(t, c):
            p = pos_ref[0, 0, t]
            xs_scr[pl.ds(p, 1), :] = hp_ref[pl.ds(t, 1), :]
            cs_scr[pl.ds(p, 1), :] = cmb_ref[pl.ds(t, 1), :]
            return c

        lax.fori_loop(0, tm, move, 0, unroll=8)

    def run_rows(r0, rows):
        lane = lax.broadcasted_iota(jnp.int32, (rows, LANES), 1)
        x = xs_scr[pl.ds(r0, rows), :].astype(BF16)
        cw = cs_scr[pl.ds(r0, rows), :]
        hid = []
        for e in range(per_split):
            a = _dot(x, wg_ref[0, e])
            u = _dot(x, wu_ref[0, e])
            col = _R_EXP0 + g * EXPERTS_PER_GROUP + hf * per_split + e
            w = jnp.sum(jnp.where(lane == col, cw, 0.0), axis=-1, keepdims=True)
            hid.append((jax.nn.silu(a) * u * w).astype(BF16))
        y = _dot(jnp.concatenate(hid, axis=1), wd_ref[0])

        @pl.when(hf == 0)
        def _():
            ys_scr[pl.ds(r0, rows), :] = y

        @pl.when(hf != 0)
        def _():
            ys_scr[pl.ds(r0, rows), :] += y

    n_blocks = meta_ref[i, g]
    start = meta_ref[i, N_GROUPS + g]
    n_big = n_blocks // big

    def big_chunk(b, c):
        run_rows(pl.multiple_of(start + b * (big * rb), rb), big * rb)
        return c

    def small_chunk(b, c):
        run_rows(pl.multiple_of(start + b * rb, rb), rb)
        return c

    lax.fori_loop(0, n_big, big_chunk, 0)
    lax.fori_loop(n_big * big, n_blocks, small_chunk, 0)

    @pl.when((g == N_GROUPS - 1) & (hf == n_split - 1))
    def _():
        def move_back(t, c):
            y_ref[pl.ds(t, 1), :] = ys_scr[pl.ds(pos_ref[0, 0, t], 1), :]
            return c

        lax.fori_loop(0, tm, move_back, 0, unroll=8)
        x2 = x1_ref[...] + g2_ref[0] * y_ref[...]
        y_ref[...] = x2 * lax.rsqrt(jnp.mean(x2 * x2, axis=-1, keepdims=True) + EPS) * gain_ref[...]


def _moe(h2p, cmb, x1, g2, gain, wg, wu, wd, tm, tiles_per_batch):
    t = h2p.shape[0]
    rb = MOE_ROW_BLOCK
    n_split = wg.shape[0] // N_GROUPS
    n_tiles = t // tm
    rows_max = tm + N_GROUPS * rb
    pos_f, meta_f = _moe_plan(cmb, tm, rb)
    pos = pos_f[:, 0].astype(jnp.int32).reshape(n_tiles, 1, tm)
    meta = jnp.concatenate([meta_f[:, 0, :N_GROUPS], meta_f[:, 1, :N_GROUPS]], axis=1).astype(jnp.int32)

    row = lambda n: pl.BlockSpec((tm, n), lambda i, g, s, m: (i, 0))
    r = g2.shape[1]
    if tiles_per_batch is None:
        g2_spec = pl.BlockSpec((1, r, D_MODEL), lambda i, g, s, m: (i, 0, 0))
    else:
        g2_spec = pl.BlockSpec((1, r, D_MODEL), lambda i, g, s, m: (i // tiles_per_batch, 0, 0))
    wmap = lambda i, g, s, m: (g * n_split + s, 0, 0, 0)
    grid_spec = pltpu.PrefetchScalarGridSpec(
        num_scalar_prefetch=1,
        grid=(n_tiles, N_GROUPS, n_split),
        in_specs=[pl.BlockSpec((1, 1, tm), lambda i, g, s, m: (i, 0, 0), memory_space=pltpu.SMEM),
                  row(D_MODEL), row(LANES), row(D_MODEL), g2_spec,
                  pl.BlockSpec((1, D_MODEL), lambda i, g, s, m: (0, 0)),
                  pl.BlockSpec((1,) + wg.shape[1:], wmap),
                  pl.BlockSpec((1,) + wu.shape[1:], wmap),
                  pl.BlockSpec((1,) + wd.shape[1:], lambda i, g, s, m: (g * n_split + s, 0, 0))],
        out_specs=row(D_MODEL),
        scratch_shapes=[pltpu.VMEM((rows_max, D_MODEL), F32), pltpu.VMEM((rows_max, LANES), F32),
                        pltpu.VMEM((rows_max, D_MODEL), F32)],
    )
    return pl.pallas_call(
        functools.partial(_moe_kernel, rb=rb, big=MOE_BIG_CHUNK, n_split=n_split),
        grid_spec=grid_spec,
        out_shape=jax.ShapeDtypeStruct((t, D_MODEL), F32),
        name="moe_final",
        compiler_params=_cparams(("parallel", "arbitrary", "arbitrary")),
    )(meta, pos, h2p, cmb, x1, g2, gain, wg, wu, wd)


def _rope_tables(pos):
    half = DK_RET // 2
    inv = ROPE_BASE ** (-jnp.arange(half, dtype=F32) / half)
    ang = pos.astype(F32)[:, None] * inv[None, :]
    cos, sin = jnp.cos(ang), jnp.sin(ang)
    zero = jnp.zeros_like(sin)
    tile = lambda a, b: jnp.tile(jnp.concatenate([a, b], axis=-1), (1, H_RET))
    return tile(cos, cos), tile(-sin, zero), tile(zero, sin)


def _pick_tile(n, target):
    t = min(n, target)
    while n % t:
        t //= 2
    return t


def kernel(x_prompt, x_sample, cache_k, cache_v, cache_logf, state_ret, page_table, c_prompt, c_sample, w_ada, b_ada, norm_mix, norm_ffn, w_in, b_fgt, w_br_fox, w_br_ret, w_out, w_route_group, b_route_group, w_route_expert, b_route_expert, w_exp_gate, w_exp_up, w_exp_down, norm_final):
    depth = w_ada.shape[0]
    assert depth == 1
    bp, seq, d = x_prompt.shape
    bd, lq, _ = x_sample.shape
    n_pool, page_size = cache_k.shape[1], cache_k.shape[2]
    n_pages = page_table.shape[1]
    past_len = n_pages * page_size
    tp, ts = bp * seq, bd * lq

    wl = w_in[0]
    w_perm = jnp.concatenate(
        [wl[:, 0:1536], wl[:, 1544:5128], wl[:, 1536:1544], jnp.zeros((d, LANES - H_FOX), F32)], axis=1).astype(BF16)
    b_fgt_pad = jnp.pad(b_fgt[0], (0, LANES - H_FOX)).reshape(1, LANES)
    gain_mix, gain_ffn, gain_fin = norm_mix[0].reshape(1, d), norm_ffn[0].reshape(1, d), norm_final.reshape(1, d)
    wf, wr, wo = w_br_fox[0].astype(BF16), w_br_ret[0].astype(BF16), w_out[0].astype(BF16)
    w_rt = jnp.pad(jnp.concatenate([w_route_group[0], w_route_expert[0]], axis=1),
                   ((0, 0), (0, LANES - N_GROUPS - N_EXPERTS)))
    wrh = w_rt.astype(BF16)
    wrl = (w_rt - wrh.astype(F32)).astype(BF16)
    b_rt = jnp.pad(jnp.concatenate([b_route_group[0], b_route_expert[0]]), (0, LANES - N_GROUPS - N_EXPERTS))
    b_rt = b_rt.reshape(1, LANES)
    per_split = EXPERTS_PER_GROUP // MOE_SPLIT
    wg = w_exp_gate[0].astype(BF16).reshape(N_GROUPS * MOE_SPLIT, per_split, d, D_EXPERT)
    wu = w_exp_up[0].astype(BF16).reshape(N_GROUPS * MOE_SPLIT, per_split, d, D_EXPERT)
    wd = w_exp_down[0].astype(BF16).reshape(N_GROUPS * MOE_SPLIT, per_split * D_EXPERT, d)

    n_c = bp + bd
    n_c_pad = -(-n_c // 8) * 8
    c_all = jnp.pad(jnp.concatenate([c_prompt, c_sample], axis=0), ((0, n_c_pad - n_c), (0, 0)))
    mod = _ada_mod(c_all, w_ada[0], b_ada[0])
    mods = [mod[:, i * d:(i + 1) * d] for i in range(6)]
    tm_p = _pick_tile(seq, 512)
    tm_s = _pick_tile(ts, 512)
    mod_p = [m[:bp].reshape(bp, 1, d) for m in mods]
    mod_s = [jnp.repeat(m[bp:n_c], lq, axis=0).reshape(ts // tm_s, tm_s, d) for m in mods]

    xp = x_prompt.reshape(tp, d)
    tpb = seq // tm_p
    cos_p, sa_p, sb_p = _rope_tables(jnp.arange(seq))
    (qt, k_wide, kt, vt, vt_chunks, lf, qb, kb, vb, rg, ga, gb) = _in_proj(
        xp, mod_p[1], mod_p[0], gain_mix, w_perm, b_fgt_pad, cos_p, sa_p, sb_p, tm_p, tpb, tpb, True)
    logf_p = lf[:, :H_FOX].reshape(bp, seq, H_FOX)

    lf_rows = logf_p.transpose(0, 2, 1).reshape(bp * H_FOX, seq // LANES, LANES)
    c3 = jnp.stack([c.reshape(bp, H_FOX, seq) for c in _cumsum_split(lf_rows)], axis=2)
    row_id = jnp.arange(_C_ROWS)[:, None]
    one_rows = ((row_id >= 3) & (row_id < 6)).astype(BF16)
    c_rows = jnp.pad(c3, ((0, 0), (0, 0), (0, _C_ROWS - 3), (0, 0))) + one_rows
    k_c = jnp.pad(-c3.transpose(0, 1, 3, 2), ((0, 0), (0, 0), (0, 0), (DH_FOX + 3, LANES - DH_FOX - 6)))
    tq = _pick_tile(seq, 512)
    yf_t = _fox_prompt(qt, c_rows, k_wide + k_c, vt_chunks, tq, 4).reshape(bp, D_FOX, seq)

    n_chunks = _pick_tile(seq // RET_CHUNK, 4)
    yr_p, rstate_p = _ret_prompt(qb, kb, vb, rg, bp, n_chunks)

    x1_p, h2_p, cmb_p = _mix(yf_t, yr_p, ga, gb, xp, mod_p[2], mod_p[4], mod_p[3], gain_ffn,
                             wf, wr, wo, wrh, wrl, b_rt, tm_p, tpb, True)
    tm_mp = _pick_tile(seq, MOE_TILE)
    y_p = _moe(h2_p, cmb_p, x1_p, mod_p[5], gain_fin, wg, wu, wd, tm_mp, seq // tm_mp)

    xs = x_sample.reshape(ts, d)
    cos_s, sa_s, sb_s = [jnp.tile(tb, (tm_s // lq, 1)) for tb in _rope_tables(past_len + jnp.arange(lq))]
    (qa_s, ka_s, va_s, lf_s, qb_s, kb_s, vb_s, rg_s, ga_s, gb_s) = _in_proj(
        xs, mod_s[1], mod_s[0], gain_mix, w_perm, b_fgt_pad, cos_s, sa_s, sb_s, tm_s, None, 1, False)
    logf_s = lf_s[:, :H_FOX].reshape(bd, lq, H_FOX)

    q4 = qa_s.reshape(bd, lq, H_FOX, DH_FOX).transpose(0, 2, 1, 3)
    eye = jnp.eye(H_FOX, dtype=BF16)
    qt_bd = (q4[:, :, :, None, :] * eye[None, :, None, :, None]).reshape(bd, H_FOX * lq, D_FOX)
    pps = _pick_tile(n_pages, 32)
    kt_pages = cache_k[0].transpose(0, 2, 3, 1).reshape(n_pool, D_FOX, page_size)
    vt_pages = cache_v[0].transpose(0, 2, 3, 1).reshape(n_pool, D_FOX, page_size)
    f_pages = _page_suffix(cache_logf[0].transpose(0, 2, 1).reshape(n_pool * H_FOX, page_size))
    yf_s = _fox_sample(page_table, kt_pages, vt_pages, f_pages,
                       qt_bd, ka_s.reshape(bd, lq, D_FOX).astype(BF16), va_s.reshape(bd, lq, D_FOX).astype(BF16),
                       logf_s.transpose(0, 2, 1), pps).reshape(ts, D_FOX)

    n_seq = _pick_tile(bd, 16)
    yr_s, rstate_s = _ret_sample(qb_s, kb_s, vb_s, rg_s, state_ret[0], lq, n_seq)

    x1_s, h2_s, cmb_s = _mix(yf_s, yr_s, ga_s, gb_s, xs, mod_s[2], mod_s[4], mod_s[3], gain_ffn,
                             wf, wr, wo, wrh, wrl, b_rt, tm_s, None, False)
    tm_ms = _pick_tile(ts, MOE_TILE)
    y_s = _moe(h2_s, cmb_s, x1_s, mod_s[5].reshape(ts // tm_ms, tm_ms, d), gain_fin, wg, wu, wd, tm_ms, None)

    return (y_p.reshape(bp, seq, d), y_s.reshape(bd, lq, d),
            kt.transpose(0, 3, 1, 2)[None], vt.transpose(0, 3, 1, 2)[None],
            logf_p[None], rstate_p[None],
            ka_s.reshape(1, bd, lq, H_FOX, DH_FOX), va_s.reshape(1, bd, lq, H_FOX, DH_FOX),
            logf_s[None], rstate_s[None])
```

```python
import functools

import numpy as np
import jax
import jax.numpy as jnp
from jax import lax
from jax.experimental import pallas as pl
from jax.experimental.pallas import tpu as pltpu

F32 = jnp.float32
BF16 = jnp.bfloat16

D_MODEL = 1024
H_FOX, DH_FOX = 8, 64
D_FOX = H_FOX * DH_FOX
H_RET, DK_RET, DV_RET = 4, 64, 128
D_RET_QK = H_RET * DK_RET
D_RET_V = H_RET * DV_RET
RET_CHUNK = 128
ROPE_BASE = 10000.0
N_GROUPS, EXPERTS_PER_GROUP, D_EXPERT = 4, 8, 256
N_EXPERTS = N_GROUPS * EXPERTS_PER_GROUP
EPS = 1e-6
LOG2E = 1.4426950408889634
NEG_BIG = -1e30
LANES = 128
KEY_STRIP = 256
V_ROWS = DH_FOX + 16

_C_QA, _C_KA, _C_VA = 0, 512, 1024
_C_QB, _C_KB, _C_VB, _C_RG, _C_GA, _C_GB, _C_FA, _C_END = 1536, 1792, 2048, 2560, 3072, 4096, 5120, 5248
_R_EXP0 = N_GROUPS
_R_GID = LANES - 1
MOE_TILE = 1024
MOE_SPLIT = 2
MOE_BIG_CHUNK = 4
MOE_ROW_BLOCK = 64

VMEM_LIMIT = 56 * 1024 * 1024


def _cparams(sem, vmem=VMEM_LIMIT, flags=None):
    return pltpu.CompilerParams(dimension_semantics=sem, vmem_limit_bytes=vmem, flags=flags)


def _split3(x):
    hi = x.astype(BF16)
    r1 = x - hi.astype(F32)
    mid = r1.astype(BF16)
    lo = (r1 - mid.astype(F32)).astype(BF16)
    return hi, mid, lo


def _dot(a, b):
    return jnp.dot(a, b, preferred_element_type=F32)


def _dot_nt(a, b):
    return lax.dot_general(a, b, (((1,), (1,)), ((), ())), preferred_element_type=F32)


def _dot_tn(a, b):
    return lax.dot_general(a, b, (((0,), (0,)), ((), ())), preferred_element_type=F32)


def _dot3_rhs_exact(x, m):
    return sum(_dot(p, m) for p in _split3(x))


def _dot3_lhs_exact(m, x):
    return sum(_dot(m, p) for p in _split3(x))


def _ada_kernel(c_ref, w_ref, b_ref, o_ref):
    a = jax.nn.silu(c_ref[...]).astype(BF16)
    o_ref[...] = _dot(a, w_ref[...].astype(BF16)) + b_ref[...]


def _ada_mod(c_all, w_ada, b_ada):
    rows = c_all.shape[0]
    n = w_ada.shape[1]
    tn = 1536
    return pl.pallas_call(
        _ada_kernel,
        grid=(n // tn,),
        in_specs=[pl.BlockSpec((rows, D_MODEL), lambda j: (0, 0)),
                  pl.BlockSpec((D_MODEL, tn), lambda j: (0, j)),
                  pl.BlockSpec((1, tn), lambda j: (0, j))],
        out_specs=pl.BlockSpec((rows, tn), lambda j: (0, j)),
        out_shape=jax.ShapeDtypeStruct((rows, n), F32),
        name="ada_mod",
        compiler_params=_cparams(("parallel",)),
    )(c_all, w_ada, b_ada.reshape(1, n))


def _inproj_kernel(x_ref, sc_ref, sh_ref, g_ref, w_ref, bf_ref, cos_ref, sa_ref, sb_ref, *out_refs, head_major):
    x = x_ref[...]
    tm = x.shape[0]
    y = x * lax.rsqrt(jnp.mean(x * x, axis=-1, keepdims=True) + EPS) * g_ref[...]
    h = (y * (1.0 + sc_ref[0]) + sh_ref[0]).astype(BF16)

    def proj(lo, hi):
        return _dot(h, w_ref[:, lo:hi])

    cos, sa, sb = cos_ref[...], sa_ref[...], sb_ref[...]

    def rope(t):
        return t * cos + pltpu.roll(t, D_RET_QK - DK_RET // 2, 1) * sa + pltpu.roll(t, DK_RET // 2, 1) * sb

    q = proj(_C_QA, _C_KA) * (DH_FOX ** -0.5 * LOG2E)
    k = proj(_C_KA, _C_VA)
    v = proj(_C_VA, _C_QB)
    if head_major:
        qt_ref, kw_ref, kt_ref, vt_ref, vc_ref = out_refs[:5]
        rest, t_scr = out_refs[5:-1], out_refs[-1]

        def transposed(a):
            t_scr[...] = a
            return t_scr[...].T

        qt_ref[0] = transposed(q).astype(BF16).reshape(H_FOX, DH_FOX, tm)
        kt_ref[0] = transposed(k).reshape(H_FOX, DH_FOX, tm)
        vt = transposed(v)
        vt_ref[0] = vt.reshape(H_FOX, DH_FOX, tm)
        vtb = vt.astype(BF16)
        lane = lax.broadcasted_iota(jnp.int32, (tm, LANES), 1)
        ones = jnp.where((lane >= DH_FOX) & (lane < DH_FOX + 3), 1.0, 0.0)
        srow = lax.broadcasted_iota(jnp.int32, (V_ROWS - DH_FOX, KEY_STRIP), 0)
        one_rows = jnp.where(srow == 0, 1.0, 0.0).astype(BF16)
        for hh in range(H_FOX):
            pair = k[:, (hh // 2) * LANES:(hh // 2 + 1) * LANES]
            if hh % 2:
                pair = pltpu.roll(pair, DH_FOX, 1)
            kw_ref[0, hh] = jnp.where(lane < DH_FOX, pair, ones).astype(BF16)
            for c in range(tm // KEY_STRIP):
                vc_ref[0, hh, c, :DH_FOX] = vtb[hh * DH_FOX:(hh + 1) * DH_FOX, c * KEY_STRIP:(c + 1) * KEY_STRIP]
                vc_ref[0, hh, c, DH_FOX:] = one_rows
    else:
        q_ref, k_ref, v_ref = out_refs[:3]
        rest = out_refs[3:]
        q_ref[...] = q.astype(BF16)
        k_ref[...] = k
        v_ref[...] = v
    lf_ref, qb_ref, kb_ref, vb_ref, rg_ref, ga_ref, gb_ref = rest
    qb_ref[...] = rope(proj(_C_QB, _C_KB)).astype(BF16)
    kb_ref[...] = (rope(proj(_C_KB, _C_VB)) * DK_RET ** -0.5).astype(BF16)
    vb_ref[...] = proj(_C_VB, _C_RG).astype(BF16)
    rg_ref[...] = proj(_C_RG, _C_GA)
    ga_ref[...] = proj(_C_GA, _C_GB)
    gb_ref[...] = proj(_C_GB, _C_FA)
    lf_ref[...] = jax.nn.log_sigmoid(proj(_C_FA, _C_END) + bf_ref[...])


def _mod_spec(mod, tiles_per_batch):
    r = mod.shape[1]
    if tiles_per_batch is None:
        return pl.BlockSpec((1, r, D_MODEL), lambda i: (i, 0, 0))
    return pl.BlockSpec((1, r, D_MODEL), lambda i: (i // tiles_per_batch, 0, 0))


def _in_proj(x, sc, sh, gain, w_perm, b_fgt_pad, cos, sa, sb, tm, tiles_per_batch, rope_tiles, head_major):
    t = x.shape[0]
    row = lambda n: pl.BlockSpec((tm, n), lambda i: (i, 0))
    const = lambda shp: pl.BlockSpec(shp, lambda i: (0,) * len(shp), pipeline_mode=pl.Buffered(1))
    rope_spec = pl.BlockSpec((tm, D_RET_QK), lambda i: (i % rope_tiles, 0))
    outs = [(LANES, F32), (D_RET_QK, BF16), (D_RET_QK, BF16), (D_RET_V, BF16), (D_RET_V, F32),
            (D_MODEL, F32), (D_MODEL, F32)]
    if head_major:
        tpb = tiles_per_batch
        nb, seq = t // (tpb * tm), tpb * tm
        tspec = pl.BlockSpec((1, H_FOX, DH_FOX, tm), lambda i: (i // tpb, 0, 0, i % tpb))
        head_specs = [tspec, pl.BlockSpec((1, H_FOX, tm, LANES), lambda i: (i // tpb, 0, i % tpb, 0)), tspec, tspec,
                      pl.BlockSpec((1, H_FOX, tm // KEY_STRIP, V_ROWS, KEY_STRIP),
                                   lambda i: (i // tpb, 0, i % tpb, 0, 0))]
        head_shapes = [jax.ShapeDtypeStruct((nb, H_FOX, DH_FOX, seq), BF16),
                       jax.ShapeDtypeStruct((nb, H_FOX, seq, LANES), BF16),
                       jax.ShapeDtypeStruct((nb, H_FOX, DH_FOX, seq), F32),
                       jax.ShapeDtypeStruct((nb, H_FOX, DH_FOX, seq), F32),
                       jax.ShapeDtypeStruct((nb, H_FOX, seq // KEY_STRIP, V_ROWS, KEY_STRIP), BF16)]
    else:
        head_specs = [row(D_FOX)] * 3
        head_shapes = [jax.ShapeDtypeStruct((t, D_FOX), dt) for dt in (BF16, F32, F32)]
    return pl.pallas_call(
        functools.partial(_inproj_kernel, head_major=head_major),
        grid=(t // tm,),
        in_specs=[row(D_MODEL), _mod_spec(sc, tiles_per_batch), _mod_spec(sh, tiles_per_batch),
                  const((1, D_MODEL)), const((D_MODEL, _C_END)), const((1, LANES)),
                  rope_spec, rope_spec, rope_spec],
        out_specs=head_specs + [row(n) for n, _ in outs],
        out_shape=head_shapes + [jax.ShapeDtypeStruct((t, n), dt) for n, dt in outs],
        scratch_shapes=[pltpu.VMEM((tm, D_FOX), F32)] if head_major else [],
        name="in_proj",
        compiler_params=_cparams(("parallel",)),
    )(x, sc, sh, gain, w_perm, b_fgt_pad, cos, sa, sb)


def _cumsum_kernel(x_ref, hi_ref, mid_ref, lo_ref):
    x = x_ref[0]
    r = x.shape[0]
    a = lax.broadcasted_iota(jnp.int32, (LANES, LANES), 0)
    b = lax.broadcasted_iota(jnp.int32, (LANES, LANES), 1)
    incl = (a <= b).astype(BF16)
    y = _dot3_rhs_exact(x, incl)
    tot = jnp.broadcast_to(y[:, LANES - 1:LANES], (r, LANES))
    ra = lax.broadcasted_iota(jnp.int32, (r, r), 0)
    rb = lax.broadcasted_iota(jnp.int32, (r, r), 1)
    strict = (rb < ra).astype(BF16)
    c = (y + _dot3_lhs_exact(strict, tot)) * LOG2E
    hi, mid, lo = _split3(c)
    hi_ref[0] = hi
    mid_ref[0] = mid
    lo_ref[0] = lo


def _cumsum_split(lf_rows):
    n, r, _ = lf_rows.shape
    spec = pl.BlockSpec((1, r, LANES), lambda i: (i, 0, 0))
    return pl.pallas_call(
        _cumsum_kernel,
        grid=(n,),
        in_specs=[spec],
        out_specs=[spec] * 3,
        out_shape=[jax.ShapeDtypeStruct((n, r, LANES), BF16)] * 3,
        name="logf_cumsum",
        compiler_params=_cparams(("parallel",)),
    )(lf_rows)


_C_ROWS = 16


def _fox_prompt_kernel(q_ref, ca_ref, k_ref, v_ref, o_ref, m_scr, acc_scr, *, tq, ks):
    i = pl.program_id(2)
    n_heads = q_ref.shape[1]
    per = tq // ks
    qts = [jnp.concatenate([q_ref[0, g], ca_ref[0, g], jnp.zeros((LANES - DH_FOX - _C_ROWS, tq), BF16)], axis=0)
           for g in range(n_heads)]

    m_scr[...] = jnp.full(m_scr.shape, NEG_BIG, F32)
    acc_scr[...] = jnp.zeros(acc_scr.shape, F32)

    def strips(c0, masked, n_strips=per):
        cs = [c0 + u for u in range(n_strips)]
        scores = [[_dot(k_ref[0, g, pl.ds(pl.multiple_of(c * ks, ks), ks), :], qts[g]) for g in range(n_heads)]
                  for c in cs]
        for u, c in enumerate(cs):
            for g in range(n_heads):
                s = scores[u][g]
                if masked:
                    kpos = c * ks + lax.broadcasted_iota(jnp.int32, (ks, tq), 0)
                    qpos = i * tq + lax.broadcasted_iota(jnp.int32, (ks, tq), 1)
                    s = jnp.where(kpos <= qpos, s, NEG_BIG)
                m = m_scr[g]
                m_new = jnp.maximum(m, jnp.max(s, axis=0, keepdims=True))
                alpha = jnp.exp2(m - m_new)
                p = jnp.exp2(s - m_new)
                acc_scr[g] = alpha * acc_scr[g] + _dot(v_ref[0, g, c], p.astype(BF16))
                m_scr[g] = m_new

    def double_block(j, carry):
        strips(j * 2 * per, False, 2 * per)
        return carry

    lax.fori_loop(0, i // 2, double_block, 0)

    @pl.when(i % 2 == 1)
    def _():
        strips((i - 1) * per, False)

    strips(i * per, True)
    for g in range(n_heads):
        o_ref[0, g] = (acc_scr[g, :DH_FOX] / acc_scr[g, DH_FOX:DH_FOX + 1]).astype(BF16)


def _fox_prompt(qt, c_rows, k_aug, vt_chunks, tq, heads_per_step):
    b, h, _, s = qt.shape
    g = heads_per_step
    ks = vt_chunks.shape[-1]
    qmap = lambda bi, hi, i: (bi, hi, 0, i)
    resident = lambda shp: pl.BlockSpec(shp, lambda bi, hi, i: (bi, hi) + (0,) * (len(shp) - 2),
                                        pipeline_mode=pl.Buffered(1))
    return pl.pallas_call(
        functools.partial(_fox_prompt_kernel, tq=tq, ks=ks),
        grid=(b, h // g, s // tq),
        in_specs=[pl.BlockSpec((1, g, DH_FOX, tq), qmap),
                  pl.BlockSpec((1, g, _C_ROWS, tq), qmap),
                  resident((1, g, s, LANES)),
                  resident((1, g, s // ks, V_ROWS, ks))],
        out_specs=pl.BlockSpec((1, g, DH_FOX, tq), qmap),
        out_shape=jax.ShapeDtypeStruct((b, h, DH_FOX, s), BF16),
        scratch_shapes=[pltpu.VMEM((g, 1, tq), F32), pltpu.VMEM((g, V_ROWS, tq), F32)],
        name="fox_prompt",
        compiler_params=_cparams(("parallel", "parallel", "arbitrary")),
    )(qt, c_rows, k_aug, vt_chunks)


def _fox_sample_kernel(pt_ref, *refs, pps, n_steps, lq, ps):
    k_refs = refs[:pps]
    v_refs = refs[pps:2 * pps]
    f_refs = refs[2 * pps:3 * pps]
    qt_ref, kn_ref, vn_ref, fn_ref, o_ref, m_scr, l_scr, acc_scr, car_scr = refs[3 * pps:]
    del pt_ref
    hq = H_FOX * lq
    step = pl.program_id(1)

    row_h = lax.broadcasted_iota(jnp.int32, (hq, H_FOX), 0) // lq
    col_h = lax.broadcasted_iota(jnp.int32, (hq, H_FOX), 1)
    rep = (row_h == col_h).astype(BF16)

    @pl.when(step == 0)
    def _():
        m_scr[...] = jnp.full(m_scr.shape, NEG_BIG, F32)
        l_scr[...] = jnp.zeros(l_scr.shape, F32)
        acc_scr[...] = jnp.zeros(acc_scr.shape, F32)
        car_scr[...] = jnp.zeros(car_scr.shape, F32)

    qt = qt_ref[0]

    def update(scores, pv_fn):
        mx = scores[0]
        for s in scores[1:]:
            mx = jnp.maximum(mx, s)
        m = m_scr[...]
        m_new = jnp.maximum(m, jnp.max(mx, axis=-1, keepdims=True))
        alpha = jnp.exp2(m - m_new)
        ps = [jnp.exp2(s - m_new) for s in scores]
        tot = ps[0]
        for p in ps[1:]:
            tot = tot + p
        l_scr[...] = alpha * l_scr[...] + jnp.sum(tot, axis=-1, keepdims=True)
        acc_scr[...] = alpha * acc_scr[...] + pv_fn([p.astype(BF16) for p in ps])
        m_scr[...] = m_new

    car = car_scr[...]
    biases = []
    for j in range(pps):
        f = f_refs[j][...]
        biases.append(f[:, :ps] + car)
        car = car + f[:, ps:]
    car_scr[...] = car
    bias_rows = _dot3_lhs_exact(rep, jnp.concatenate(biases, axis=1))
    scores = [_dot(qt, k_refs[j][0].astype(BF16)) + bias_rows[:, j * ps:(j + 1) * ps] for j in range(pps)]

    def pv_pages(ps_bf):
        pv = _dot_nt(ps_bf[0], v_refs[0][0].astype(BF16))
        for j in range(1, pps):
            pv = pv + _dot_nt(ps_bf[j], v_refs[j][0].astype(BF16))
        return pv

    update(scores, pv_pages)

    @pl.when(step == n_steps - 1)
    def _():
        na = lax.broadcasted_iota(jnp.int32, (lq, lq), 0)
        nb = lax.broadcasted_iota(jnp.int32, (lq, lq), 1)
        c_new = _dot3_rhs_exact(fn_ref[0], (na <= nb).astype(BF16)) * LOG2E
        cj = _dot3_lhs_exact(rep, c_new)
        qi = lax.broadcasted_iota(jnp.int32, (hq, lq), 0) % lq
        kj = lax.broadcasted_iota(jnp.int32, (hq, lq), 1)
        s = jnp.where(kj <= qi, _dot_nt(qt, kn_ref[0]) - cj, NEG_BIG)
        update([s], lambda ps_bf: _dot(ps_bf[0], vn_ref[0]))
        acc = acc_scr[...] / l_scr[...]
        rh = lax.broadcasted_iota(jnp.int32, (hq, D_FOX), 0) // lq
        ch = lax.broadcasted_iota(jnp.int32, (hq, D_FOX), 1) // DH_FOX
        acc = jnp.where(rh == ch, acc, 0.0)
        out = acc[0:lq]
        for h in range(1, H_FOX):
            out = out + acc[h * lq:(h + 1) * lq]
        o_ref[0] = out.astype(BF16)


def _fox_sample(page_table, kt_pages, vt_pages, f_pages, qt_bd, k_new, v_new, f_new_t, pps):
    bd, n_pages = page_table.shape
    lq = k_new.shape[1]
    hq = H_FOX * lq
    n_steps = n_pages // pps
    ps = kt_pages.shape[2]

    def page_map(j):
        return lambda b, p, pt: (pt[b, n_pages - 1 - (p * pps + j)], 0, 0)

    def page_map2(j):
        return lambda b, p, pt: (pt[b, n_pages - 1 - (p * pps + j)], 0)

    per_b = lambda shp: pl.BlockSpec((1,) + shp, lambda b, p, pt: (b, 0, 0))
    in_specs = ([pl.BlockSpec((1, D_FOX, ps), page_map(j)) for j in range(pps)]
                + [pl.BlockSpec((1, D_FOX, ps), page_map(j)) for j in range(pps)]
                + [pl.BlockSpec((H_FOX, 2 * ps), page_map2(j)) for j in range(pps)]
                + [per_b((hq, D_FOX)), per_b((lq, D_FOX)), per_b((lq, D_FOX)), per_b((H_FOX, lq))])
    grid_spec = pltpu.PrefetchScalarGridSpec(
        num_scalar_prefetch=1,
        grid=(bd, n_steps),
        in_specs=in_specs,
        out_specs=per_b((lq, D_FOX)),
        scratch_shapes=[pltpu.VMEM((hq, 1), F32), pltpu.VMEM((hq, 1), F32),
                        pltpu.VMEM((hq, D_FOX), F32), pltpu.VMEM((H_FOX, ps), F32)],
    )
    return pl.pallas_call(
        functools.partial(_fox_sample_kernel, pps=pps, n_steps=n_steps, lq=lq, ps=ps),
        grid_spec=grid_spec,
        out_shape=jax.ShapeDtypeStruct((bd, lq, D_FOX), BF16),
        name="fox_sample",
        compiler_params=_cparams(("parallel", "arbitrary")),
    )(page_table, *([kt_pages] * pps), *([vt_pages] * pps), *([f_pages] * pps), qt_bd, k_new, v_new, f_new_t)


def _page_suffix_kernel(x_ref, o_ref):
    x = x_ref[...]
    ps = x.shape[1]
    ja = lax.broadcasted_iota(jnp.int32, (ps, ps), 0)
    jb = lax.broadcasted_iota(jnp.int32, (ps, ps), 1)
    o_ref[:, :ps] = _dot3_rhs_exact(x, (ja > jb).astype(BF16)) * LOG2E
    o_ref[:, ps:] = _dot3_rhs_exact(x, jnp.ones((ps, ps), BF16)) * LOG2E


def _page_suffix(lf_rows):
    n, ps = lf_rows.shape
    tr = _pick_tile(n, 4096)
    return pl.pallas_call(
        _page_suffix_kernel,
        grid=(n // tr,),
        in_specs=[pl.BlockSpec((tr, ps), lambda i: (i, 0))],
        out_specs=pl.BlockSpec((tr, 2 * ps), lambda i: (i, 0)),
        out_shape=jax.ShapeDtypeStruct((n, 2 * ps), F32),
        name="page_suffix",
        compiler_params=_cparams(("parallel",)),
    )(lf_rows)


def _ret_head(q, k, v, rg, r, dec, qd, kd, cd):
    mm = q.dtype
    inner = _dot_nt(q, k) * dec
    o = _dot(inner.astype(mm), v) + qd * _dot(q, r.astype(mm))
    r_new = cd * r + _dot_tn((k.astype(F32) * kd).astype(mm), v)
    mu = jnp.mean(o, axis=-1, keepdims=True)
    var = jnp.mean(jnp.square(o - mu), axis=-1, keepdims=True)
    y = (o - mu) * lax.rsqrt(var + EPS)
    return (y * jax.nn.silu(rg)).astype(BF16), r_new


def _ret_prompt_kernel(q_ref, k_ref, v_ref, rg_ref, dec_ref, qd_ref, kd_ref, cd_ref, y_ref, rout_ref, r_scr,
                       *, chunk, n_chunks):
    step = pl.program_id(1)

    @pl.when(step == 0)
    def _():
        r_scr[...] = jnp.zeros(r_scr.shape, F32)

    for c in range(n_chunks):
        rows = slice(c * chunk, (c + 1) * chunk)
        for h in range(H_RET):
            qk = slice(h * DK_RET, (h + 1) * DK_RET)
            vv = slice(h * DV_RET, (h + 1) * DV_RET)
            y, r_new = _ret_head(q_ref[rows, qk], k_ref[rows, qk], v_ref[rows, vv], rg_ref[rows, vv], r_scr[h],
                                 dec_ref[h], qd_ref[h], kd_ref[h], cd_ref[h])
            y_ref[rows, vv] = y
            r_scr[h] = r_new

    @pl.when(step == pl.num_programs(1) - 1)
    def _():
        rout_ref[0] = r_scr[...]


def _ret_tables(chunk):
    lg = jnp.log1p(-jnp.exp2(-5.0 - jnp.arange(H_RET, dtype=F32)))
    pos = jnp.arange(chunk, dtype=F32)
    diff = pos[:, None] - pos[None, :]
    dec = jnp.where(diff >= 0, jnp.exp(lg[:, None, None] * jnp.maximum(diff, 0.0)), 0.0)
    q_dec = jnp.exp(lg[:, None] * (pos + 1.0))
    k_dec = jnp.exp(lg[:, None] * (chunk - 1.0 - pos))
    chunk_dec = jnp.exp(lg * chunk)
    qd = jnp.broadcast_to(q_dec[:, :, None], (H_RET, chunk, DV_RET))
    kd = jnp.broadcast_to(k_dec[:, :, None], (H_RET, chunk, DK_RET))
    cd = jnp.broadcast_to(chunk_dec[:, None, None], (H_RET, DK_RET, DV_RET))
    return dec, qd, kd, cd


def _ret_prompt(qb, kb, vb, rg, batch, n_chunks):
    t = qb.shape[0]
    chunk = RET_CHUNK
    rows = chunk * n_chunks
    steps = t // batch // rows
    row = lambda n: pl.BlockSpec((rows, n), lambda b, i: (b * steps + i, 0))
    const = lambda shp: pl.BlockSpec(shp, lambda b, i: (0,) * len(shp))
    tables = _ret_tables(chunk)
    return pl.pallas_call(
        functools.partial(_ret_prompt_kernel, chunk=chunk, n_chunks=n_chunks),
        grid=(batch, steps),
        in_specs=[row(D_RET_QK), row(D_RET_QK), row(D_RET_V), row(D_RET_V)] + [const(tb.shape) for tb in tables],
        out_specs=[row(D_RET_V), pl.BlockSpec((1, H_RET, DK_RET, DV_RET), lambda b, i: (b, 0, 0, 0))],
        out_shape=[jax.ShapeDtypeStruct((t, D_RET_V), BF16),
                   jax.ShapeDtypeStruct((batch, H_RET, DK_RET, DV_RET), F32)],
        scratch_shapes=[pltpu.VMEM((H_RET, DK_RET, DV_RET), F32)],
        name="ret_prompt",
        compiler_params=_cparams(("parallel", "arbitrary")),
    )(qb, kb, vb, rg, *tables)


def _ret_sample_kernel(q_ref, k_ref, v_ref, rg_ref, r0_ref, dec_ref, qd_ref, kd_ref, cd_ref, y_ref, rout_ref,
                       *, chunk, n_seq):
    for s in range(n_seq):
        rows = slice(s * chunk, (s + 1) * chunk)
        for h in range(H_RET):
            qk = slice(h * DK_RET, (h + 1) * DK_RET)
            vv = slice(h * DV_RET, (h + 1) * DV_RET)
            y, r_new = _ret_head(q_ref[rows, qk].astype(F32), k_ref[rows, qk].astype(F32),
                                 v_ref[rows, vv].astype(F32), rg_ref[rows, vv], r0_ref[s, h],
                                 dec_ref[h], qd_ref[h], kd_ref[h], cd_ref[h])
            y_ref[rows, vv] = y
            rout_ref[s, h] = r_new


def _ret_sample(qb, kb, vb, rg, r0, chunk, n_seq):
    t = qb.shape[0]
    rows = chunk * n_seq
    row = lambda n: pl.BlockSpec((rows, n), lambda i: (i, 0))
    const = lambda shp: pl.BlockSpec(shp, lambda i: (0,) * len(shp))
    st = pl.BlockSpec((n_seq, H_RET, DK_RET, DV_RET), lambda i: (i, 0, 0, 0))
    tables = _ret_tables(chunk)
    return pl.pallas_call(
        functools.partial(_ret_sample_kernel, chunk=chunk, n_seq=n_seq),
        grid=(t // rows,),
        in_specs=[row(D_RET_QK), row(D_RET_QK), row(D_RET_V), row(D_RET_V), st] + [const(tb.shape) for tb in tables],
        out_specs=[row(D_RET_V), st],
        out_shape=[jax.ShapeDtypeStruct((t, D_RET_V), BF16), jax.ShapeDtypeStruct(r0.shape, F32)],
        name="ret_sample",
        compiler_params=_cparams(("parallel",)),
    )(qb, kb, vb, rg, r0, *tables)


def _route(lg):
    lane = lax.broadcasted_iota(jnp.int32, lg.shape, 1)
    big = jnp.int32(1 << 20)
    is_grp = lane < N_GROUPS
    mg = jnp.max(jnp.where(is_grp, lg, -jnp.inf), axis=-1, keepdims=True)
    g_idx = jnp.min(jnp.where(is_grp & (lg == mg), lane, big), axis=-1, keepdims=True)
    g_w = 1.0 / jnp.sum(jnp.where(is_grp, jnp.exp(lg - mg), 0.0), axis=-1, keepdims=True)
    lo = _R_EXP0 + g_idx * EXPERTS_PER_GROUP
    sel = (lane >= lo) & (lane < lo + EXPERTS_PER_GROUP)
    v1 = jnp.max(jnp.where(sel, lg, -jnp.inf), axis=-1, keepdims=True)
    i1 = jnp.min(jnp.where(sel & (lg == v1), lane, big), axis=-1, keepdims=True)
    sel2 = sel & (lane != i1)
    v2 = jnp.max(jnp.where(sel2, lg, -jnp.inf), axis=-1, keepdims=True)
    i2 = jnp.min(jnp.where(sel2 & (lg == v2), lane, big), axis=-1, keepdims=True)
    e2 = jnp.exp(v2 - v1)
    w1 = g_w / (1.0 + e2)
    w2 = g_w * e2 / (1.0 + e2)
    cmb = jnp.where(lane == i1, w1, 0.0) + jnp.where(lane == i2, w2, 0.0)
    return jnp.where(lane == _R_GID, g_idx.astype(F32), cmb)


def _mix_kernel(yf_ref, yr_ref, ga_ref, gb_ref, x_ref, g1_ref, sc_ref, sh_ref, gain_ref,
                wf_ref, wr_ref, wo_ref, wrh_ref, wrl_ref, br_ref,
                x1_ref, h2_ref, cmb_ref, *, fox_transposed):
    if fox_transposed:
        pf = _dot_tn(yf_ref[0], wf_ref[...])
    else:
        pf = _dot(yf_ref[...], wf_ref[...])
    pr = _dot(yr_ref[...], wr_ref[...])
    mixed = jax.nn.sigmoid(ga_ref[...]) * pf + jax.nn.sigmoid(gb_ref[...]) * pr
    x1 = x_ref[...] + g1_ref[0] * _dot(mixed.astype(BF16), wo_ref[...])
    x1_ref[...] = x1
    y = x1 * lax.rsqrt(jnp.mean(x1 * x1, axis=-1, keepdims=True) + EPS) * gain_ref[...]
    h2 = y * (1.0 + sc_ref[0]) + sh_ref[0]
    h2_ref[...] = h2
    h_hi = h2.astype(BF16)
    h_lo = (h2 - h_hi.astype(F32)).astype(BF16)
    lg = _dot(h_hi, wrh_ref[...]) + (_dot(h_hi, wrl_ref[...]) + _dot(h_lo, wrh_ref[...])) + br_ref[...]
    cmb_ref[...] = _route(lg)


def _mix(yf, yr, ga, gb, x, g1, sc2, sh2, gain, wf, wr, wo, wrh, wrl, br, tm, tiles_per_batch, fox_transposed):
    t = x.shape[0]
    row = lambda n: pl.BlockSpec((tm, n), lambda i: (i, 0))
    const = lambda shp: pl.BlockSpec(shp, lambda i: (0,) * len(shp))
    if fox_transposed:
        yf_spec = pl.BlockSpec((1, D_FOX, tm), lambda i: (i // tiles_per_batch, 0, i % tiles_per_batch))
    else:
        yf_spec = row(D_FOX)
    ms = lambda m: _mod_spec(m, tiles_per_batch)
    return pl.pallas_call(
        functools.partial(_mix_kernel, fox_transposed=fox_transposed),
        grid=(t // tm,),
        in_specs=[yf_spec, row(D_RET_V), row(D_MODEL), row(D_MODEL), row(D_MODEL), ms(g1), ms(sc2), ms(sh2),
                  const((1, D_MODEL)), const((D_FOX, D_MODEL)), const((D_RET_V, D_MODEL)),
                  const((D_MODEL, D_MODEL)), const((D_MODEL, LANES)), const((D_MODEL, LANES)), const((1, LANES))],
        out_specs=[row(D_MODEL), row(D_MODEL), row(LANES)],
        out_shape=[jax.ShapeDtypeStruct((t, D_MODEL), F32), jax.ShapeDtypeStruct((t, D_MODEL), F32),
                   jax.ShapeDtypeStruct((t, LANES), F32)],
        name="mix_route",
        compiler_params=_cparams(("parallel",)),
    )(yf, yr, ga, gb, x, g1, sc2, sh2, gain, wf, wr, wo, wrh, wrl, br)


def _moe_plan_kernel(cmb_ref, pos_ref, meta_ref, p_scr, *, rb):
    tm = cmb_ref.shape[0]
    blk = min(256, tm)
    lane = lax.broadcasted_iota(jnp.int32, (blk, LANES), 1).astype(F32)
    ra = lax.broadcasted_iota(jnp.int32, (blk, blk), 0)
    rbi = lax.broadcasted_iota(jnp.int32, (blk, blk), 1)
    before = (rbi < ra).astype(BF16)
    run = jnp.zeros((1, LANES), F32)
    hots = []
    for b in range(tm // blk):
        gid = cmb_ref[b * blk:(b + 1) * blk, _R_GID:_R_GID + 1]
        hot = jnp.where(lane == gid, 1.0, 0.0)
        hots.append(hot)
        p_scr[b * blk:(b + 1) * blk, :] = _dot(before, hot.astype(BF16)) + run
        run = run + jnp.sum(hot, axis=0, keepdims=True)
    n_blocks = jnp.floor((run + (rb - 1)) * (1.0 / rb))
    la = lax.broadcasted_iota(jnp.int32, (LANES, LANES), 0)
    lb = lax.broadcasted_iota(jnp.int32, (LANES, LANES), 1)
    start = _dot(n_blocks.astype(BF16), (la < lb).astype(BF16)) * rb
    for b in range(tm // blk):
        pos = jnp.sum(hots[b] * (p_scr[b * blk:(b + 1) * blk, :] + start), axis=-1, keepdims=True)
        pos_ref[b * blk:(b + 1) * blk, :] = jnp.broadcast_to(pos, (blk, LANES))
    row = lax.broadcasted_iota(jnp.int32, (8, LANES), 0)
    meta_ref[0] = jnp.where(row == 0, n_blocks, jnp.where(row == 1, start, 0.0))


def _moe_plan(cmb, tm, rb):
    t = cmb.shape[0]
    return pl.pallas_call(
        functools.partial(_moe_plan_kernel, rb=rb),
        grid=(t // tm,),
        in_specs=[pl.BlockSpec((tm, LANES), lambda i: (i, 0))],
        out_specs=[pl.BlockSpec((tm, LANES), lambda i: (i, 0)), pl.BlockSpec((1, 8, LANES), lambda i: (i, 0, 0))],
        out_shape=[jax.ShapeDtypeStruct((t, LANES), F32), jax.ShapeDtypeStruct((t // tm, 8, LANES), F32)],
        scratch_shapes=[pltpu.VMEM((tm, LANES), F32)],
        name="moe_plan",
        compiler_params=_cparams(("parallel",)),
    )(cmb)


def _moe_kernel(meta_ref, pos_ref, hp_ref, cmb_ref, x1_ref, g2_ref, gain_ref, wg_ref, wu_ref, wd_ref, y_ref,
                xs_scr, cs_scr, ys_scr, *, rb, big, n_split):
    i, g, hf = pl.program_id(0), pl.program_id(1), pl.program_id(2)
    tm = hp_ref.shape[0]
    per_split = EXPERTS_PER_GROUP // n_split

    @pl.when((g == 0) & (hf == 0))
    def _():
        xs_scr[...] = jnp.zeros(xs_scr.shape, F32)
        cs_scr[...] = jnp.zeros(cs_scr.shape, F32)

        def move(t, c):
            p = pos_ref[0, 0, t]
            xs_scr[pl.ds(p, 1), :] = hp_ref[pl.ds(t, 1), :]
            cs_scr[pl.ds(p, 1), :] = cmb_ref[pl.ds(t, 1), :]
            return c

        lax.fori_loop(0, tm, move, 0, unroll=8)

    def run_rows(r0, rows):
        lane = lax.broadcasted_iota(jnp.int32, (rows, LANES), 1)
        x = xs_scr[pl.ds(r0, rows), :].astype(BF16)
        cw = cs_scr[pl.ds(r0, rows), :]
        hid = []
        for e in range(per_split):
            a = _dot(x, wg_ref[0, e])
            u = _dot(x, wu_ref[0, e])
            col = _R_EXP0 + g * EXPERTS_PER_GROUP + hf * per_split + e
            w = jnp.sum(jnp.where(lane == col, cw, 0.0), axis=-1, keepdims=True)
            hid.append((jax.nn.silu(a) * u * w).astype(BF16))
        y = _dot(jnp.concatenate(hid, axis=1), wd_ref[0])

        @pl.when(hf == 0)
        def _():
            ys_scr[pl.ds(r0, rows), :] = y

        @pl.when(hf != 0)
        def _():
            ys_scr[pl.ds(r0, rows), :] += y

    n_blocks = meta_ref[i, g]
    start = meta_ref[i, N_GROUPS + g]
    n_big = n_blocks // big

    def big_chunk(b, c):
        run_rows(pl.multiple_of(start + b * (big * rb), rb), big * rb)
        return c

    def small_chunk(b, c):
        run_rows(pl.multiple_of(start + b * rb, rb), rb)
        return c

    lax.fori_loop(0, n_big, big_chunk, 0)
    lax.fori_loop(n_big * big, n_blocks, small_chunk, 0)

    @pl.when((g == N_GROUPS - 1) & (hf == n_split - 1))
    def _():
        def move_back(t, c):
            y_ref[pl.ds(t, 1), :] = ys_scr[pl.ds(pos_ref[0, 0, t], 1), :]
            return c

        lax.fori_loop(0, tm, move_back, 0, unroll=8)
        x2 = x1_ref[...] + g2_ref[0] * y_ref[...]
        y_ref[...] = x2 * lax.rsqrt(jnp.mean(x2 * x2, axis=-1, keepdims=True) + EPS) * gain_ref[...]


def _moe(h2p, cmb, x1, g2, gain, wg, wu, wd, tm, tiles_per_batch):
    t = h2p.shape[0]
    rb = MOE_ROW_BLOCK
    n_split = wg.shape[0] // N_GROUPS
    n_tiles = t // tm
    rows_max = tm + N_GROUPS * rb
    pos_f, meta_f = _moe_plan(cmb, tm, rb)
    pos = pos_f[:, 0].astype(jnp.int32).reshape(n_tiles, 1, tm)
    meta = jnp.concatenate([meta_f[:, 0, :N_GROUPS], meta_f[:, 1, :N_GROUPS]], axis=1).astype(jnp.int32)

    row = lambda n: pl.BlockSpec((tm, n), lambda i, g, s, m: (i, 0))
    r = g2.shape[1]
    if tiles_per_batch is None:
        g2_spec = pl.BlockSpec((1, r, D_MODEL), lambda i, g, s, m: (i, 0, 0))
    else:
        g2_spec = pl.BlockSpec((1, r, D_MODEL), lambda i, g, s, m: (i // tiles_per_batch, 0, 0))
    wmap = lambda i, g, s, m: (g * n_split + s, 0, 0, 0)
    grid_spec = pltpu.PrefetchScalarGridSpec(
        num_scalar_prefetch=1,
        grid=(n_tiles, N_GROUPS, n_split),
        in_specs=[pl.BlockSpec((1, 1, tm), lambda i, g, s, m: (i, 0, 0), memory_space=pltpu.SMEM),
                  row(D_MODEL), row(LANES), row(D_MODEL), g2_spec,
                  pl.BlockSpec((1, D_MODEL), lambda i, g, s, m: (0, 0)),
                  pl.BlockSpec((1,) + wg.shape[1:], wmap),
                  pl.BlockSpec((1,) + wu.shape[1:], wmap),
                  pl.BlockSpec((1,) + wd.shape[1:], lambda i, g, s, m: (g * n_split + s, 0, 0))],
        out_specs=row(D_MODEL),
        scratch_shapes=[pltpu.VMEM((rows_max, D_MODEL), F32), pltpu.VMEM((rows_max, LANES), F32),
                        pltpu.VMEM((rows_max, D_MODEL), F32)],
    )
    return pl.pallas_call(
        functools.partial(_moe_kernel, rb=rb, big=MOE_BIG_CHUNK, n_split=n_split),
        grid_spec=grid_spec,
        out_shape=jax.ShapeDtypeStruct((t, D_MODEL), F32),
        name="moe_final",
        compiler_params=_cparams(("parallel", "arbitrary", "arbitrary")),
    )(meta, pos, h2p, cmb, x1, g2, gain, wg, wu, wd)


def _rope_tables(pos):
    half = DK_RET // 2
    inv = ROPE_BASE ** (-jnp.arange(half, dtype=F32) / half)
    ang = pos.astype(F32)[:, None] * inv[None, :]
    cos, sin = jnp.cos(ang), jnp.sin(ang)
    zero = jnp.zeros_like(sin)
    tile = lambda a, b: jnp.tile(jnp.concatenate([a, b], axis=-1), (1, H_RET))
    return tile(cos, cos), tile(-sin, zero), tile(zero, sin)


def _pick_tile(n, target):
    t = min(n, target)
    while n % t:
        t //= 2
    return t


def kernel(x_prompt, x_sample, cache_k, cache_v, cache_logf, state_ret, page_table, c_prompt, c_sample, w_ada, b_ada, norm_mix, norm_ffn, w_in, b_fgt, w_br_fox, w_br_ret, w_out, w_route_group, b_route_group, w_route_expert, b_route_expert, w_exp_gate, w_exp_up, w_exp_down, norm_final):
    depth = w_ada.shape[0]
    assert depth == 1
    bp, seq, d = x_prompt.shape
    bd, lq, _ = x_sample.shape
    n_pool, page_size = cache_k.shape[1], cache_k.shape[2]
    n_pages = page_table.shape[1]
    past_len = n_pages * page_size
    tp, ts = bp * seq, bd * lq

    wl = w_in[0]
    w_perm = jnp.concatenate(
        [wl[:, 0:1536], wl[:, 1544:5128], wl[:, 1536:1544], jnp.zeros((d, LANES - H_FOX), F32)], axis=1).astype(BF16)
    b_fgt_pad = jnp.pad(b_fgt[0], (0, LANES - H_FOX)).reshape(1, LANES)
    gain_mix, gain_ffn, gain_fin = norm_mix[0].reshape(1, d), norm_ffn[0].reshape(1, d), norm_final.reshape(1, d)
    wf, wr, wo = w_br_fox[0].astype(BF16), w_br_ret[0].astype(BF16), w_out[0].astype(BF16)
    w_rt = jnp.pad(jnp.concatenate([w_route_group[0], w_route_expert[0]], axis=1),
                   ((0, 0), (0, LANES - N_GROUPS - N_EXPERTS)))
    wrh = w_rt.astype(BF16)
    wrl = (w_rt - wrh.astype(F32)).astype(BF16)
    b_rt = jnp.pad(jnp.concatenate([b_route_group[0], b_route_expert[0]]), (0, LANES - N_GROUPS - N_EXPERTS))
    b_rt = b_rt.reshape(1, LANES)
    per_split = EXPERTS_PER_GROUP // MOE_SPLIT
    wg = w_exp_gate[0].astype(BF16).reshape(N_GROUPS * MOE_SPLIT, per_split, d, D_EXPERT)
    wu = w_exp_up[0].astype(BF16).reshape(N_GROUPS * MOE_SPLIT, per_split, d, D_EXPERT)
    wd = w_exp_down[0].astype(BF16).reshape(N_GROUPS * MOE_SPLIT, per_split * D_EXPERT, d)

    n_c = bp + bd
    n_c_pad = -(-n_c // 8) * 8
    c_all = jnp.pad(jnp.concatenate([c_prompt, c_sample], axis=0), ((0, n_c_pad - n_c), (0, 0)))
    mod = _ada_mod(c_all, w_ada[0], b_ada[0])
    mods = [mod[:, i * d:(i + 1) * d] for i in range(6)]
    tm_p = _pick_tile(seq, 512)
    tm_s = _pick_tile(ts, 512)
    mod_p = [m[:bp].reshape(bp, 1, d) for m in mods]
    mod_s = [jnp.repeat(m[bp:n_c], lq, axis=0).reshape(ts // tm_s, tm_s, d) for m in mods]

    xp = x_prompt.reshape(tp, d)
    tpb = seq // tm_p
    cos_p, sa_p, sb_p = _rope_tables(jnp.arange(seq))
    (qt, k_wide, kt, vt, vt_chunks, lf, qb, kb, vb, rg, ga, gb) = _in_proj(
        xp, mod_p[1], mod_p[0], gain_mix, w_perm, b_fgt_pad, cos_p, sa_p, sb_p, tm_p, tpb, tpb, True)
    logf_p = lf[:, :H_FOX].reshape(bp, seq, H_FOX)

    lf_rows = logf_p.transpose(0, 2, 1).reshape(bp * H_FOX, seq // LANES, LANES)
    c3 = jnp.stack([c.reshape(bp, H_FOX, seq) for c in _cumsum_split(lf_rows)], axis=2)
    c_rows = jnp.concatenate([c3, jnp.ones((bp, H_FOX, 3, seq), BF16),
                              jnp.zeros((bp, H_FOX, _C_ROWS - 6, seq), BF16)], axis=2)
    k_c = jnp.pad(-c3.transpose(0, 1, 3, 2), ((0, 0), (0, 0), (0, 0), (DH_FOX + 3, LANES - DH_FOX - 6)))
    tq = _pick_tile(seq, 512)
    yf_t = _fox_prompt(qt, c_rows, k_wide + k_c, vt_chunks, tq, 4).reshape(bp, D_FOX, seq)

    n_chunks = _pick_tile(seq // RET_CHUNK, 4)
    yr_p, rstate_p = _ret_prompt(qb, kb, vb, rg, bp, n_chunks)

    x1_p, h2_p, cmb_p = _mix(yf_t, yr_p, ga, gb, xp, mod_p[2], mod_p[4], mod_p[3], gain_ffn,
                             wf, wr, wo, wrh, wrl, b_rt, tm_p, tpb, True)
    tm_mp = _pick_tile(seq, MOE_TILE)
    y_p = _moe(h2_p, cmb_p, x1_p, mod_p[5], gain_fin, wg, wu, wd, tm_mp, seq // tm_mp)

    xs = x_sample.reshape(ts, d)
    cos_s, sa_s, sb_s = [jnp.tile(tb, (tm_s // lq, 1)) for tb in _rope_tables(past_len + jnp.arange(lq))]
    (qa_s, ka_s, va_s, lf_s, qb_s, kb_s, vb_s, rg_s, ga_s, gb_s) = _in_proj(
        xs, mod_s[1], mod_s[0], gain_mix, w_perm, b_fgt_pad, cos_s, sa_s, sb_s, tm_s, None, 1, False)
    logf_s = lf_s[:, :H_FOX].reshape(bd, lq, H_FOX)

    q4 = qa_s.reshape(bd, lq, H_FOX, DH_FOX).transpose(0, 2, 1, 3)
    eye = jnp.eye(H_FOX, dtype=BF16)
    qt_bd = (q4[:, :, :, None, :] * eye[None, :, None, :, None]).reshape(bd, H_FOX * lq, D_FOX)
    pps = _pick_tile(n_pages, 32)
    kt_pages = cache_k[0].transpose(0, 2, 3, 1).reshape(n_pool, D_FOX, page_size)
    vt_pages = cache_v[0].transpose(0, 2, 3, 1).reshape(n_pool, D_FOX, page_size)
    f_pages = _page_suffix(cache_logf[0].transpose(0, 2, 1).reshape(n_pool * H_FOX, page_size))
    yf_s = _fox_sample(page_table, kt_pages, vt_pages, f_pages,
                       qt_bd, ka_s.reshape(bd, lq, D_FOX).astype(BF16), va_s.reshape(bd, lq, D_FOX).astype(BF16),
                       logf_s.transpose(0, 2, 1), pps).reshape(ts, D_FOX)

    n_seq = _pick_tile(bd, 16)
    yr_s, rstate_s = _ret_sample(qb_s, kb_s, vb_s, rg_s, state_ret[0], lq, n_seq)

    x1_s, h2_s, cmb_s = _mix(yf_s, yr_s, ga_s, gb_s, xs, mod_s[2], mod_s[4], mod_s[3], gain_ffn,
                             wf, wr, wo, wrh, wrl, b_rt, tm_s, None, False)
    tm_ms = _pick_tile(ts, MOE_TILE)
    y_s = _moe(h2_s, cmb_s, x1_s, mod_s[5].reshape(ts // tm_ms, tm_ms, d), gain_fin, wg, wu, wd, tm_ms, None)

    return (y_p.reshape(bp, seq, d), y_s.reshape(bd, lq, d),
            kt.transpose(0, 3, 1, 2)[None], vt.transpose(0, 3, 1, 2)[None],
            logf_p[None], rstate_p[None],
            ka_s.reshape(1, bd, lq, H_FOX, DH_FOX), va_s.reshape(1, bd, lq, H_FOX, DH_FOX),
            logf_s[None], rstate_s[None])
```
